```python
import jax, jax.numpy as jnp
from jax import lax
import numpy as np

D_MODEL = 1024
BATCH = 16
SEQ = 2048
DEPTH = 1

RMS_EPS = 1e-6
NEG_INF = -1e30

SSD_EXPAND = 2
SSD_D_INNER = SSD_EXPAND * D_MODEL
SSD_HEAD_DIM = 64
SSD_N_HEADS = SSD_D_INNER // SSD_HEAD_DIM
SSD_N_GROUPS = 8
SSD_D_STATE = 128
SSD_CONV = 4
SSD_CHUNK = 128
SSD_GN = SSD_N_GROUPS * SSD_D_STATE
SSD_CONV_DIM = SSD_D_INNER + 2 * SSD_GN

ATT_HEAD_DIM = 64
ATT_N_HEADS = 16
ATT_WIDTH = ATT_N_HEADS * ATT_HEAD_DIM
MOBA_BLOCK = 256
MOBA_TOPK = 3
ATT_Q_BLOCK = 128
ROPE_THETA = 500000.0
ROT_DIM = ATT_HEAD_DIM // 4

FFN_HIDDEN = 2816
FFN_CONV = 3

IN_SIZES = (SSD_D_INNER, SSD_CONV_DIM, SSD_N_HEADS, ATT_WIDTH, ATT_WIDTH, ATT_WIDTH, 2 * D_MODEL)
IN_DIM = int(sum(IN_SIZES))
IN_OFFSETS = tuple(int(v) for v in np.cumsum(IN_SIZES)[:-1])

kernel_name = "hybrid_ssd_moba_convffn_block"


def rmsnorm(x, w):
    xf = x.astype(jnp.float32)
    y = xf * lax.rsqrt(jnp.mean(xf * xf, axis=-1, keepdims=True) + RMS_EPS)
    return (y * w.astype(jnp.float32)).astype(x.dtype)


def causal_dwconv(x, w, b):
    k, c = w.shape
    y = lax.conv_general_dilated(x, w[:, None, :].astype(x.dtype), window_strides=(1,),
                                 padding=[(k - 1, 0)], dimension_numbers=('NWC', 'WIO', 'NWC'),
                                 feature_group_count=c)
    return y + b.astype(x.dtype)


def segsum(a):
    t = a.shape[-1]
    cs = jnp.cumsum(a, axis=-1)
    diff = cs[..., :, None] - cs[..., None, :]
    mask = jnp.tril(jnp.ones((t, t), dtype=bool))
    return jnp.where(mask, diff, -jnp.inf)


def ssd_scan(X, dt, A, Bm, Cm):
    b, s = X.shape[:2]
    c, l, g = s // SSD_CHUNK, SSD_CHUNK, SSD_N_GROUPS
    r = SSD_N_HEADS // g
    Xc = (X * dt[..., None]).reshape(b, c, l, g, r, SSD_HEAD_DIM)
    Adt = (dt * A).reshape(b, c, l, g, r).transpose(0, 3, 4, 1, 2)
    Bc = Bm.reshape(b, c, l, g, SSD_D_STATE)
    Cc = Cm.reshape(b, c, l, g, SSD_D_STATE)
    A_cs = jnp.cumsum(Adt, axis=-1)
    L = jnp.exp(segsum(Adt))
    CB = jnp.einsum('bclgn,bcsgn->bcgls', Cc, Bc)
    y_diag = jnp.einsum('bcgls,bgrcls,bcsgrp->bclgrp', CB, L, Xc)
    decay_states = jnp.exp(A_cs[..., -1:] - A_cs)
    states = jnp.einsum('bclgn,bgrcl,bclgrp->bcgrpn', Bc, decay_states, Xc)
    states = jnp.concatenate([jnp.zeros_like(states[:, :1]), states], axis=1)
    chunk_a = jnp.pad(A_cs[..., -1], ((0, 0), (0, 0), (0, 0), (1, 0)))
    decay_chunk = jnp.exp(segsum(chunk_a))
    states = jnp.einsum('bgrzc,bcgrpn->bzgrpn', decay_chunk, states)[:, :-1]
    y_off = jnp.einsum('bclgn,bcgrpn,bgrcl->bclgrp', Cc, states, jnp.exp(A_cs))
    return (y_diag + y_off).reshape(b, s, SSD_N_HEADS, SSD_HEAD_DIM)


def ssd_mixer(z, xbc_raw, dt_raw, conv_w, conv_b, dt_bias, a_log, d_skip, norm_w):
    b, s, _ = z.shape
    xbc = jax.nn.silu(causal_dwconv(xbc_raw, conv_w, conv_b))
    xs, Bm, Cm = jnp.split(xbc, [SSD_D_INNER, SSD_D_INNER + SSD_GN], axis=-1)
    dt = jax.nn.softplus(dt_raw.astype(jnp.float32) + dt_bias.astype(jnp.float32))
    A = -jnp.exp(a_log.astype(jnp.float32))
    X = xs.reshape(b, s, SSD_N_HEADS, SSD_HEAD_DIM).astype(jnp.float32)
    y = ssd_scan(X, dt, A,
                 Bm.reshape(b, s, SSD_N_GROUPS, SSD_D_STATE).astype(jnp.float32),
                 Cm.reshape(b, s, SSD_N_GROUPS, SSD_D_STATE).astype(jnp.float32))
    y = (y + X * d_skip.astype(jnp.float32)[:, None]).reshape(b, s, SSD_D_INNER)
    y = y * jax.nn.silu(z.astype(jnp.float32))
    yg = y.reshape(b, s, SSD_N_GROUPS, SSD_D_INNER // SSD_N_GROUPS)
    yg = yg * lax.rsqrt(jnp.mean(yg * yg, axis=-1, keepdims=True) + RMS_EPS)
    y = yg.reshape(b, s, SSD_D_INNER) * norm_w.astype(jnp.float32)
    return y.astype(z.dtype)


def partial_rope(t, pos):
    inv_freq = jnp.float32(ROPE_THETA) ** (-jnp.arange(0, ROT_DIM, 2, dtype=jnp.float32) / ROT_DIM)
    ang = pos[:, None] * inv_freq[None, :]
    ang = jnp.concatenate([ang, ang], axis=-1)
    cos = jnp.cos(ang)[None, :, None, :].astype(t.dtype)
    sin = jnp.sin(ang)[None, :, None, :].astype(t.dtype)
    rot, rest = t[..., :ROT_DIM], t[..., ROT_DIM:]
    r1, r2 = rot[..., :ROT_DIM // 2], rot[..., ROT_DIM // 2:]
    rot_half = jnp.concatenate([-r2, r1], axis=-1)
    return jnp.concatenate([rot * cos + rot_half * sin, rest], axis=-1)


def moba_attention(q, k, v):
    b, s, h, hd = q.shape
    scale = hd ** -0.5
    q, k, v = (t.transpose(0, 2, 1, 3) for t in (q, k, v))
    nb = -(-s // MOBA_BLOCK)
    s_pad = nb * MOBA_BLOCK
    kp = jnp.pad(k, ((0, 0), (0, 0), (0, s_pad - s), (0, 0)))
    vp = jnp.pad(v, ((0, 0), (0, 0), (0, s_pad - s), (0, 0)))
    k_blocks = kp.reshape(b, h, nb, MOBA_BLOCK, hd)
    v_blocks = vp.reshape(b, h, nb, MOBA_BLOCK, hd)
    k_mean = jnp.mean(k_blocks.astype(jnp.float32), axis=3)
    gate = jnp.einsum('bhsd,bhnd->bhsn', q.astype(jnp.float32), k_mean)
    q_blk = jnp.arange(s) // MOBA_BLOCK
    past = jnp.arange(nb)[None, :] < q_blk[:, None]
    gate = jnp.where(past[None, None], gate, -jnp.inf)
    n_sel = min(MOBA_TOPK, nb)
    _, sel_idx = lax.top_k(gate, n_sel)
    nq = s // ATT_Q_BLOCK
    gather_blocks = jax.vmap(lambda blocks, ix: blocks[ix])

    def attend(i):
        bi, ci = i // nq, i % nq
        q0 = ci * ATT_Q_BLOCK
        blk = q0 // MOBA_BLOCK
        qc = lax.dynamic_slice_in_dim(q[bi], q0, ATT_Q_BLOCK, axis=1).astype(jnp.float32)
        idx = lax.dynamic_slice_in_dim(sel_idx[bi], q0, ATT_Q_BLOCK, axis=1)
        k_sel = gather_blocks(k_blocks[bi], idx).astype(jnp.float32)
        v_sel = gather_blocks(v_blocks[bi], idx).astype(jnp.float32)
        s_sel = jnp.einsum('hqd,hqjkd->hqjk', qc, k_sel) * scale
        valid = (jnp.arange(n_sel) < blk)[None, None, :, None]
        s_sel = jnp.where(valid, s_sel, NEG_INF)
        k_own = lax.dynamic_slice_in_dim(kp[bi], blk * MOBA_BLOCK, MOBA_BLOCK, axis=1).astype(jnp.float32)
        v_own = lax.dynamic_slice_in_dim(vp[bi], blk * MOBA_BLOCK, MOBA_BLOCK, axis=1).astype(jnp.float32)
        s_own = jnp.einsum('hqd,hkd->hqk', qc, k_own) * scale
        causal = (blk * MOBA_BLOCK + jnp.arange(MOBA_BLOCK))[None, :] <= (q0 + jnp.arange(ATT_Q_BLOCK))[:, None]
        s_own = jnp.where(causal[None], s_own, NEG_INF)
        scores = jnp.concatenate([s_sel.reshape(h, ATT_Q_BLOCK, n_sel * MOBA_BLOCK), s_own], axis=-1)
        p = jax.nn.softmax(scores, axis=-1)
        p_sel = p[..., :n_sel * MOBA_BLOCK].reshape(h, ATT_Q_BLOCK, n_sel, MOBA_BLOCK)
        p_own = p[..., n_sel * MOBA_BLOCK:]
        out = jnp.einsum('hqjk,hqjkd->hqd', p_sel, v_sel) + jnp.einsum('hqk,hkd->hqd', p_own, v_own)
        return out.astype(v.dtype)

    outs = lax.map(attend, jnp.arange(b * nq))
    outs = outs.reshape(b, nq, h, ATT_Q_BLOCK, hd).transpose(0, 1, 3, 2, 4)
    return outs.reshape(b, s, h * hd)


def hybrid_layer(x, norm1_w, w_in, b_gate, ssd_conv_w, ssd_conv_b, ssd_dt_bias, ssd_a_log, ssd_d,
                 ssd_norm_w, w_ssd_proj, w_att_proj, w_out, norm2_w, w_ffn_up, ffn_conv_w, ffn_conv_b,
                 w_ffn_down):
    b, s, _ = x.shape
    hn = rmsnorm(x, norm1_w)
    proj = hn @ w_in
    z, xbc_raw, dt_raw, q, k, v, g_raw = jnp.split(proj, IN_OFFSETS, axis=-1)
    y_ssd = ssd_mixer(z, xbc_raw, dt_raw, ssd_conv_w, ssd_conv_b, ssd_dt_bias, ssd_a_log, ssd_d, ssd_norm_w)
    pos = jnp.arange(s, dtype=jnp.float32)
    q = partial_rope(q.reshape(b, s, ATT_N_HEADS, ATT_HEAD_DIM), pos)
    k = partial_rope(k.reshape(b, s, ATT_N_HEADS, ATT_HEAD_DIM), pos)
    y_att = moba_attention(q, k, v.reshape(b, s, ATT_N_HEADS, ATT_HEAD_DIM))
    gates = jax.nn.sigmoid((g_raw + b_gate).astype(jnp.float32)).astype(x.dtype)
    g_ssd, g_att = jnp.split(gates, 2, axis=-1)
    merged = g_ssd * (y_ssd @ w_ssd_proj) + g_att * (y_att @ w_att_proj)
    x = x + merged @ w_out
    h2 = rmsnorm(x, norm2_w)
    a, gb = jnp.split(h2 @ w_ffn_up, 2, axis=-1)
    a = causal_dwconv(a, ffn_conv_w, ffn_conv_b)
    return x + (jax.nn.silu(a) * gb) @ w_ffn_down


def setup_inputs(seed: int = 0) -> dict:
    key = jax.random.key(seed)
    ks = jax.random.split(key, 20)
    f32 = jnp.float32

    def nrm(k, shape, scale):
        return jax.random.normal(k, shape, f32) * scale

    L = DEPTH
    dt0 = jnp.exp(jax.random.uniform(ks[5], (L, SSD_N_HEADS), f32, np.log(1e-3), np.log(1e-1)))
    return {
        "x": nrm(ks[0], (BATCH, SEQ, D_MODEL), 1.0),
        "norm1_w": 1.0 + nrm(ks[1], (L, D_MODEL), 0.02),
        "w_in": nrm(ks[2], (L, D_MODEL, IN_DIM), D_MODEL ** -0.5),
        "b_gate": nrm(ks[3], (L, 2 * D_MODEL), 0.01),
        "ssd_conv_w": nrm(ks[4], (L, SSD_CONV, SSD_CONV_DIM), SSD_CONV ** -0.5),
        "ssd_conv_b": nrm(ks[6], (L, SSD_CONV_DIM), 0.01),
        "ssd_dt_bias": dt0 + jnp.log(-jnp.expm1(-dt0)),
        "ssd_a_log": jnp.log(jax.random.uniform(ks[7], (L, SSD_N_HEADS), f32, 1.0, 16.0)),
        "ssd_d": 1.0 + nrm(ks[8], (L, SSD_N_HEADS), 0.01),
        "ssd_norm_w": 1.0 + nrm(ks[9], (L, SSD_D_INNER), 0.02),
        "w_ssd_proj": nrm(ks[10], (L, SSD_D_INNER, D_MODEL), SSD_D_INNER ** -0.5),
        "w_att_proj": nrm(ks[11], (L, ATT_WIDTH, D_MODEL), ATT_WIDTH ** -0.5),
        "w_out": nrm(ks[12], (L, D_MODEL, D_MODEL), D_MODEL ** -0.5),
        "norm2_w": 1.0 + nrm(ks[13], (L, D_MODEL), 0.02),
        "w_ffn_up": nrm(ks[14], (L, D_MODEL, 2 * FFN_HIDDEN), D_MODEL ** -0.5),
        "ffn_conv_w": nrm(ks[15], (L, FFN_CONV, FFN_HIDDEN), FFN_CONV ** -0.5),
        "ffn_conv_b": nrm(ks[16], (L, FFN_HIDDEN), 0.01),
        "w_ffn_down": nrm(ks[17], (L, FFN_HIDDEN, D_MODEL), FFN_HIDDEN ** -0.5),
        "norm_f_w": 1.0 + nrm(ks[18], (D_MODEL,), 0.02),
    }


def reference(x, norm1_w, w_in, b_gate, ssd_conv_w, ssd_conv_b, ssd_dt_bias, ssd_a_log, ssd_d, ssd_norm_w,
              w_ssd_proj, w_att_proj, w_out, norm2_w, w_ffn_up, ffn_conv_w, ffn_conv_b, w_ffn_down, norm_f_w):
    for l in range(DEPTH):
        x = hybrid_layer(x, norm1_w[l], w_in[l], b_gate[l], ssd_conv_w[l], ssd_conv_b[l], ssd_dt_bias[l],
                         ssd_a_log[l], ssd_d[l], ssd_norm_w[l], w_ssd_proj[l], w_att_proj[l], w_out[l],
                         norm2_w[l], w_ffn_up[l], ffn_conv_w[l], ffn_conv_b[l], w_ffn_down[l])
    return rmsnorm(x, norm_f_w)
```

```python
import functools

import jax
import jax.numpy as jnp
import numpy as np
from jax import lax
from jax.experimental import pallas as pl
from jax.experimental.pallas import tpu as pltpu

D_MODEL = 1024
RMS_EPS = 1e-6
NEG_INF = -1e30

SSD_D_INNER = 2048
SSD_HEAD_DIM = 64
SSD_N_HEADS = 32
SSD_N_GROUPS = 8
SSD_HEADS_PER_GROUP = SSD_N_HEADS // SSD_N_GROUPS
SSD_GROUP_WIDTH = SSD_D_INNER // SSD_N_GROUPS
SSD_D_STATE = 128
SSD_CONV = 4
SSD_CHUNK = 128
SSD_GN = SSD_N_GROUPS * SSD_D_STATE
SSD_CONV_DIM = SSD_D_INNER + 2 * SSD_GN

ATT_HEAD_DIM = 64
ATT_N_HEADS = 16
ATT_WIDTH = ATT_N_HEADS * ATT_HEAD_DIM
MOBA_BLOCK = 256
MOBA_TOPK = 3
ROPE_THETA = 500000.0
ROT_DIM = ATT_HEAD_DIM // 4

FFN_HIDDEN = 2816
FFN_CONV = 3

IN_SIZES = (SSD_D_INNER, SSD_CONV_DIM, SSD_N_HEADS, ATT_WIDTH, ATT_WIDTH, ATT_WIDTH, 2 * D_MODEL)
IN_OFFSETS = tuple(int(v) for v in np.cumsum(IN_SIZES)[:-1])

LANES = 128
SUBLANES = 8
VMEM_LIMIT_BIG = 56 * 1024 * 1024
VMEM_LIMIT_SMALL = 40 * 1024 * 1024

ROW_TILE = 256
COL_CHUNK = 512

BF16 = jnp.bfloat16
F32 = jnp.float32


def _dot(a, b):
    return jnp.dot(a, b, preferred_element_type=F32)


def _dot_nt(a, b):
    return lax.dot_general(a, b, (((1,), (1,)), ((), ())), preferred_element_type=F32)


def _dot_tn(a, b):
    return lax.dot_general(a, b, (((0,), (0,)), ((), ())), preferred_element_type=F32)


def _sigmoid(x):
    return 1.0 / (1.0 + jnp.exp(-x))


def _softplus(x):
    return jnp.maximum(x, 0.0) + jnp.log1p(jnp.exp(-jnp.abs(x)))


def _split3(x):
    hi = x.astype(BF16)
    r = x - hi.astype(F32)
    mid = r.astype(BF16)
    lo = (r - mid.astype(F32)).astype(BF16)
    return hi, mid, lo


def _resident(shape):
    nd = len(shape)
    return pl.BlockSpec(shape, lambda *_: (0,) * nd, pipeline_mode=pl.Buffered(1))


def _inproj_kernel(x_ref, n1_ref, wz_ref, wxbc_ref, wdt_ref, wdtT_ref, wq_ref, wk_ref, wv_ref, wg_ref,
                   bg_ref, cw_ref, cb_ref, dtb_ref, dtbT_ref, cos_ref, sa_ref, sb_ref,
                   sz_ref, xbc_ref, dt_ref, dtT_ref, q_ref, k_ref, v_ref, g_ref, raw_ref):
    tm = x_ref.shape[0]
    halo = SUBLANES

    @pl.when(pl.program_id(1) == 0)
    def _():
        raw_ref[0:halo, :] = jnp.zeros((halo, SSD_CONV_DIM), F32)

    xf = x_ref[...]
    var = jnp.mean(xf * xf, axis=-1, keepdims=True)
    hn = (xf * lax.rsqrt(var + RMS_EPS) * n1_ref[...]).astype(BF16)

    for c in range(0, SSD_D_INNER, COL_CHUNK):
        acc = _dot(hn, wz_ref[:, c:c + COL_CHUNK])
        sz_ref[:, c:c + COL_CHUNK] = (acc * _sigmoid(acc)).astype(BF16)

    for c in range(0, SSD_CONV_DIM, COL_CHUNK):
        cs = slice(c, c + COL_CHUNK)
        raw_ref[halo:halo + tm, cs] = _dot(hn, wxbc_ref[:, cs])
        conv = cb_ref[:, cs]
        for j in range(SSD_CONV):
            off = halo - (SSD_CONV - 1) + j
            conv = conv + cw_ref[j:j + 1, cs] * raw_ref[off:off + tm, cs]
        xbc_ref[:, cs] = (conv * _sigmoid(conv)).astype(BF16)
        raw_ref[0:halo, cs] = raw_ref[tm:tm + halo, cs]

    dt_ref[...] = _softplus(_dot(hn, wdt_ref[...]) + dtb_ref[...])
    dtT_ref[...] = _softplus(_dot_nt(wdtT_ref[...], hn) + dtbT_ref[...])

    reps = COL_CHUNK // LANES
    cos = jnp.concatenate([cos_ref[...]] * reps, axis=1)
    sa = jnp.concatenate([sa_ref[...]] * reps, axis=1)
    sb = jnp.concatenate([sb_ref[...]] * reps, axis=1)
    half = ROT_DIM // 2
    for w_ref, o_ref in ((wq_ref, q_ref), (wk_ref, k_ref)):
        for c in range(0, ATT_WIDTH, COL_CHUNK):
            acc = _dot(hn, w_ref[:, c:c + COL_CHUNK])
            nxt = pltpu.roll(acc, COL_CHUNK - half, axis=1)
            prv = pltpu.roll(acc, half, axis=1)
            o_ref[:, c:c + COL_CHUNK] = (acc * cos + nxt * sa + prv * sb).astype(BF16)
    for c in range(0, ATT_WIDTH, COL_CHUNK):
        v_ref[:, c:c + COL_CHUNK] = _dot(hn, wv_ref[:, c:c + COL_CHUNK]).astype(BF16)

    for c in range(0, 2 * D_MODEL, COL_CHUNK):
        acc = _dot(hn, wg_ref[:, c:c + COL_CHUNK]) + bg_ref[:, c:c + COL_CHUNK]
        g_ref[:, c:c + COL_CHUNK] = _sigmoid(acc).astype(BF16)


def _rope_tables(s):
    half = ROT_DIM // 2
    pos = jnp.arange(s, dtype=F32)
    inv_freq = jnp.float32(ROPE_THETA) ** (-jnp.arange(0, ROT_DIM, 2, dtype=F32) / ROT_DIM)
    ang = pos[:, None] * inv_freq[None, :]
    ang = jnp.concatenate([ang, ang], axis=-1)
    cos, sin = jnp.cos(ang), jnp.sin(ang)
    pad = ATT_HEAD_DIM - ROT_DIM
    cos_t = jnp.concatenate([cos, jnp.ones((s, pad), F32)], axis=-1)
    sa_t = jnp.concatenate([-sin[:, :half], jnp.zeros((s, ATT_HEAD_DIM - half), F32)], axis=-1)
    sb_t = jnp.concatenate([jnp.zeros((s, half), F32), sin[:, half:], jnp.zeros((s, pad), F32)], axis=-1)
    reps = LANES // ATT_HEAD_DIM
    return tuple(jnp.tile(t, (1, reps)) for t in (cos_t, sa_t, sb_t))


def _in_projection(x, norm1_w, w_in, b_gate, conv_w, conv_b, dt_bias):
    b, s, d = x.shape
    tm = ROW_TILE
    wz, wxbc, wdt, wq, wk, wv, wg = (w.astype(BF16) for w in jnp.split(w_in, IN_OFFSETS, axis=-1))
    cos_t, sa_t, sb_t = _rope_tables(s)
    row = lambda bi, ti: (bi, ti, 0)
    tab = lambda bi, ti: (ti, 0)

    def out(width, dtype=BF16):
        return jax.ShapeDtypeStruct((b, s, width), dtype)

    in_specs = [
        pl.BlockSpec((None, tm, d), row),
        _resident((1, d)),
        _resident(wz.shape), _resident(wxbc.shape), _resident(wdt.shape), _resident((SSD_N_HEADS, d)),
        _resident(wq.shape), _resident(wk.shape), _resident(wv.shape), _resident(wg.shape),
        _resident((1, 2 * D_MODEL)), _resident((SSD_CONV, SSD_CONV_DIM)), _resident((1, SSD_CONV_DIM)),
        _resident((1, SSD_N_HEADS)), _resident((SSD_N_HEADS, 1)),
        pl.BlockSpec((tm, LANES), tab), pl.BlockSpec((tm, LANES), tab), pl.BlockSpec((tm, LANES), tab),
    ]
    out_specs = [
        pl.BlockSpec((None, tm, SSD_D_INNER), row),
        pl.BlockSpec((None, tm, SSD_CONV_DIM), row),
        pl.BlockSpec((None, tm, SSD_N_HEADS), row),
        pl.BlockSpec((None, SSD_N_HEADS, tm), lambda bi, ti: (bi, 0, ti)),
        pl.BlockSpec((None, tm, ATT_WIDTH), row),
        pl.BlockSpec((None, tm, ATT_WIDTH), row),
        pl.BlockSpec((None, tm, ATT_WIDTH), row),
        pl.BlockSpec((None, tm, 2 * D_MODEL), row),
    ]
    out_shape = [out(SSD_D_INNER), out(SSD_CONV_DIM), out(SSD_N_HEADS, F32),
                 jax.ShapeDtypeStruct((b, SSD_N_HEADS, s), F32),
                 out(ATT_WIDTH), out(ATT_WIDTH), out(ATT_WIDTH), out(2 * D_MODEL)]
    return pl.pallas_call(
        _inproj_kernel,
        grid=(b, s // tm),
        in_specs=in_specs,
        out_specs=out_specs,
        out_shape=out_shape,
        scratch_shapes=[pltpu.VMEM((tm + SUBLANES, SSD_CONV_DIM), F32)],
        compiler_params=pltpu.CompilerParams(
            dimension_semantics=("arbitrary", "arbitrary"), vmem_limit_bytes=VMEM_LIMIT_BIG),
        name="in_projection",
    )(x, norm1_w.reshape(1, d), wz, wxbc, wdt, wdt.T, wq, wk, wv, wg,
      b_gate.reshape(1, -1), conv_w, conv_b.reshape(1, -1),
      dt_bias.reshape(1, -1), dt_bias.reshape(-1, 1), cos_t, sa_t, sb_t)


def _ssd_kernel(xbc_ref, sz_ref, dt_ref, dtT_ref, alog_ref, alogT_ref, dsk_ref, nw_ref, y_ref, state_ref):
    l = SSD_CHUNK
    gw = SSD_GROUP_WIDTH
    hpg = SSD_HEADS_PER_GROUP
    p = SSD_HEAD_DIM

    @pl.when(pl.program_id(1) == 0)
    def _():
        state_ref[...] = jnp.zeros(state_ref.shape, F32)

    dt = dt_ref[...]
    a = dt * (-jnp.exp(alog_ref[...]))
    aT = dtT_ref[...] * (-jnp.exp(alogT_ref[...]))

    ri = lax.broadcasted_iota(jnp.int32, (l, l), 0)
    ci = lax.broadcasted_iota(jnp.int32, (l, l), 1)
    lower = ri >= ci
    tri = jnp.where(lower, 1.0, 0.0).astype(BF16)
    triT = jnp.where(ri <= ci, 1.0, 0.0).astype(BF16)
    cs = sum(_dot(tri, piece) for piece in _split3(a))
    csT = sum(_dot(piece, triT) for piece in _split3(aT))
    tot = cs[l - 1:l, :]
    ecs = jnp.exp(cs)
    dtdec = dt * jnp.exp(tot - cs)
    etot = jnp.exp(tot)

    def per_group(v, g, rows):
        return jnp.concatenate(
            [jnp.broadcast_to(v[:, g * hpg + r:g * hpg + r + 1], (rows, p)) for r in range(hpg)], axis=1)

    lane_head = lax.broadcasted_iota(jnp.int32, (l, gw), 1) // p

    for g in range(SSD_N_GROUPS):
        xg = xbc_ref[:, g * gw:(g + 1) * gw].astype(F32)
        bg = xbc_ref[:, SSD_D_INNER + g * SSD_D_STATE:SSD_D_INNER + (g + 1) * SSD_D_STATE]
        cg = xbc_ref[:, SSD_D_INNER + SSD_GN + g * SSD_D_STATE:
                     SSD_D_INNER + SSD_GN + (g + 1) * SSD_D_STATE]
        cb = _dot_nt(cg, bg)
        xdt = xg * per_group(dt, g, l)
        xdec = (xg * per_group(dtdec, g, l)).astype(BF16)

        y = xg * dsk_ref[:, g * gw:(g + 1) * gw]
        for r in range(hpg):
            h = g * hpg + r
            decay = jnp.exp(cs[:, h:h + 1] - csT[h:h + 1, :])
            m = jnp.where(lower, cb * decay, 0.0).astype(BF16)
            y = y + _dot(m, jnp.where(lane_head == r, xdt, 0.0).astype(BF16))

        st = state_ref[g]
        y = y + _dot(cg, st.astype(BF16)) * per_group(ecs, g, l)
        state_ref[g] = st * per_group(etot, g, 1) + _dot_tn(bg, xdec)

        y = y * sz_ref[:, g * gw:(g + 1) * gw].astype(F32)
        ms = jnp.mean(y * y, axis=-1, keepdims=True)
        y = y * lax.rsqrt(ms + RMS_EPS) * nw_ref[:, g * gw:(g + 1) * gw]
        y_ref[:, g * gw:(g + 1) * gw] = y.astype(BF16)


def _ssd(xbc, sz, dt, dtT, a_log, d_skip, norm_w):
    b, s, _ = xbc.shape
    l = SSD_CHUNK
    row = lambda bi, ci: (bi, ci, 0)
    return pl.pallas_call(
        _ssd_kernel,
        grid=(b, s // l),
        in_specs=[
            pl.BlockSpec((None, l, SSD_CONV_DIM), row),
            pl.BlockSpec((None, l, SSD_D_INNER), row),
            pl.BlockSpec((None, l, SSD_N_HEADS), row),
            pl.BlockSpec((None, SSD_N_HEADS, l), lambda bi, ci: (bi, 0, ci)),
            _resident((1, SSD_N_HEADS)), _resident((SSD_N_HEADS, 1)),
            _resident((1, SSD_D_INNER)), _resident((1, SSD_D_INNER)),
        ],
        out_specs=pl.BlockSpec((None, l, SSD_D_INNER), row),
        out_shape=jax.ShapeDtypeStruct((b, s, SSD_D_INNER), BF16),
        scratch_shapes=[pltpu.VMEM((SSD_N_GROUPS, SSD_D_STATE, SSD_GROUP_WIDTH), F32)],
        compiler_params=pltpu.CompilerParams(
            dimension_semantics=("arbitrary", "arbitrary"), vmem_limit_bytes=VMEM_LIMIT_SMALL),
        name="ssd_scan",
    )(xbc, sz, dt, dtT, a_log.reshape(1, -1), a_log.reshape(-1, 1),
      jnp.repeat(d_skip, SSD_HEAD_DIM).reshape(1, -1), norm_w.reshape(1, -1))


def _moba_kernel(q_ref, k_ref, v_ref, o_ref, ka_ref, va_ref, km_ref):
    tq = q_ref.shape[0]
    s = k_ref.shape[0]
    nb = s // MOBA_BLOCK
    nbp = km_ref.shape[0]
    hd = ATT_HEAD_DIM
    blk = pl.program_id(2)

    @pl.when(blk == 0)
    def _():
        kk = k_ref[...]
        kf = kk.astype(F32)
        vf = v_ref[...].astype(F32)
        lane = lax.broadcasted_iota(jnp.int32, (s, LANES), 1)
        kblk = lax.broadcasted_iota(jnp.int32, (s, LANES), 0) // MOBA_BLOCK
        ka_ref[0] = jnp.where(lane < hd, kf, jnp.where(lane - hd == kblk, 1.0, 0.0)).astype(BF16)
        ka_ref[1] = jnp.where(lane >= hd, kf, jnp.where(lane == kblk, 1.0, 0.0)).astype(BF16)
        va_ref[0] = jnp.where(lane < hd, vf, 1.0).astype(BF16)
        va_ref[1] = jnp.where(lane >= hd, vf, 1.0).astype(BF16)
        arow = lax.broadcasted_iota(jnp.int32, (nbp, s), 0)
        acol = lax.broadcasted_iota(jnp.int32, (nbp, s), 1) // MOBA_BLOCK
        avg = jnp.where(arow == acol, 1.0 / MOBA_BLOCK, 0.0).astype(BF16)
        km_ref[...] = _dot(avg, kk)

    q = q_ref[...]
    lane_q = lax.broadcasted_iota(jnp.int32, (tq, LANES), 1)
    lane_m = lax.broadcasted_iota(jnp.int32, (nbp, LANES), 1)
    n_idx = lax.broadcasted_iota(jnp.int32, (nbp, tq), 0)
    eye = jnp.where(lax.broadcasted_iota(jnp.int32, (tq, tq), 0) ==
                    lax.broadcasted_iota(jnp.int32, (tq, tq), 1), 1.0, 0.0).astype(BF16)
    causal = (lax.broadcasted_iota(jnp.int32, (tq, MOBA_BLOCK), 0) >=
              lax.broadcasted_iota(jnp.int32, (tq, MOBA_BLOCK), 1))
    scale = hd ** -0.5
    outs = []
    for h in range(2):
        own = (lane_q < hd) if h == 0 else (lane_q >= hd)
        own_m = (lane_m < hd) if h == 0 else (lane_m >= hd)
        km = jnp.where(own_m, km_ref[...], 0.0)
        km_hi = km.astype(BF16)
        km_lo = (km - km_hi.astype(F32)).astype(BF16)
        gt = _dot_nt(km_hi, q) + _dot_nt(km_lo, q)
        past = n_idx < blk
        rank = jnp.zeros((nbp, tq), jnp.int32)
        for m in range(nb):
            gm = gt[m:m + 1, :]
            beats = (gm > gt) | ((gm == gt) & (n_idx > m))
            rank = rank + jnp.where(beats & (blk > m), 1, 0)
        drop = past & (rank >= MOBA_TOPK)
        bias_t = jnp.where(drop, NEG_INF, 0.0).astype(BF16)
        lane0 = hd if h == 0 else 0
        pieces = [jnp.zeros((lane0, tq), BF16)] if lane0 else []
        pieces += [bias_t, jnp.zeros((LANES - nbp - lane0, tq), BF16)]
        bias_cols = _dot_nt(eye, jnp.concatenate(pieces, axis=0))
        q_aug = jnp.where(own, q.astype(F32) * scale, bias_cols).astype(BF16)

        def attend(n, m_prev, acc, mask):
            start = pl.multiple_of(n * MOBA_BLOCK, MOBA_BLOCK)
            sc = _dot_nt(q_aug, ka_ref[h, pl.ds(start, MOBA_BLOCK), :])
            if mask is not None:
                sc = jnp.where(mask, sc, NEG_INF)
            m_new = jnp.maximum(m_prev, jnp.max(sc, axis=-1, keepdims=True))
            pexp = jnp.exp(sc - m_new).astype(BF16)
            acc = acc * jnp.exp(m_prev - m_new) + _dot(pexp, va_ref[h, pl.ds(start, MOBA_BLOCK), :])
            return m_new, acc

        m0 = jnp.full((tq, 1), NEG_INF, F32)
        acc0 = jnp.zeros((tq, LANES), F32)
        m1, acc1 = attend(blk, m0, acc0, causal)
        _, acc = lax.fori_loop(0, blk, lambda n, c: attend(n, c[0], c[1], None), (m1, acc1))
        outs.append(acc / pltpu.roll(acc, hd, axis=1))
    o_ref[...] = jnp.where(lane_q < hd, outs[0], outs[1]).astype(BF16)


def _moba(q, k, v):
    b, s, _ = q.shape
    nb = s // MOBA_BLOCK
    nbp = -(-nb // SUBLANES) * SUBLANES
    tq = MOBA_BLOCK
    return pl.pallas_call(
        _moba_kernel,
        grid=(b, ATT_N_HEADS // 2, nb),
        in_specs=[
            pl.BlockSpec((None, tq, LANES), lambda bi, hi, qi: (bi, qi, hi)),
            pl.BlockSpec((None, s, LANES), lambda bi, hi, qi: (bi, 0, hi)),
            pl.BlockSpec((None, s, LANES), lambda bi, hi, qi: (bi, 0, hi)),
        ],
        out_specs=pl.BlockSpec((None, tq, LANES), lambda bi, hi, qi: (bi, qi, hi)),
        out_shape=jax.ShapeDtypeStruct((b, s, ATT_WIDTH), BF16),
        scratch_shapes=[pltpu.VMEM((2, s, LANES), BF16), pltpu.VMEM((2, s, LANES), BF16),
                        pltpu.VMEM((nbp, LANES), F32)],
        compiler_params=pltpu.CompilerParams(
            dimension_semantics=("arbitrary", "arbitrary", "arbitrary"), vmem_limit_bytes=VMEM_LIMIT_SMALL),
        name="moba_attention",
    )(q, k, v)


def _mixer_kernel(x_ref, ys_ref, ya_ref, g_ref, wsp_ref, wap_ref, wo_ref, n2_ref, wup_ref, fcw_ref, fcb_ref,
                  wdn_ref, nf_ref, o_ref, up_ref):
    tm = x_ref.shape[0]
    halo = SUBLANES

    @pl.when(pl.program_id(1) == 0)
    def _():
        up_ref[0:halo, :] = jnp.zeros((halo, FFN_HIDDEN), F32)

    merged = (g_ref[:, :D_MODEL].astype(F32) * _dot(ys_ref[...], wsp_ref[...]) +
              g_ref[:, D_MODEL:].astype(F32) * _dot(ya_ref[...], wap_ref[...]))
    x1 = x_ref[...] + _dot(merged.astype(BF16), wo_ref[...])

    var = jnp.mean(x1 * x1, axis=-1, keepdims=True)
    h2 = (x1 * lax.rsqrt(var + RMS_EPS) * n2_ref[...]).astype(BF16)

    n_chunk = FFN_HIDDEN // 2
    y = x1
    for c in range(0, FFN_HIDDEN, n_chunk):
        cs = slice(c, c + n_chunk)
        up_ref[halo:halo + tm, cs] = _dot(h2, wup_ref[:, cs])
        conv = fcb_ref[:, cs]
        for j in range(FFN_CONV):
            off = halo - (FFN_CONV - 1) + j
            conv = conv + fcw_ref[j:j + 1, cs] * up_ref[off:off + tm, cs]
        gate = _dot(h2, wup_ref[:, FFN_HIDDEN + c:FFN_HIDDEN + c + n_chunk])
        act = (conv * _sigmoid(conv) * gate).astype(BF16)
        up_ref[0:halo, cs] = up_ref[tm:tm + halo, cs]
        y = y + _dot(act, wdn_ref[cs, :])

    var = jnp.mean(y * y, axis=-1, keepdims=True)
    o_ref[...] = y * lax.rsqrt(var + RMS_EPS) * nf_ref[...]


def _mixer(x, y_ssd, y_att, gates, w_ssd_proj, w_att_proj, w_out, norm2_w, w_ffn_up, ffn_conv_w, ffn_conv_b,
           w_ffn_down, norm_f_w):
    b, s, d = x.shape
    tm = ROW_TILE
    row = lambda bi, ti: (bi, ti, 0)
    wsp, wap, wo, wup, wdn = (w.astype(BF16) for w in (w_ssd_proj, w_att_proj, w_out, w_ffn_up, w_ffn_down))
    return pl.pallas_call(
        _mixer_kernel,
        grid=(b, s // tm),
        in_specs=[
            pl.BlockSpec((None, tm, d), row),
            pl.BlockSpec((None, tm, SSD_D_INNER), row),
            pl.BlockSpec((None, tm, ATT_WIDTH), row),
            pl.BlockSpec((None, tm, 2 * D_MODEL), row),
            _resident(wsp.shape), _resident(wap.shape), _resident(wo.shape), _resident((1, d)),
            _resident(wup.shape), _resident((FFN_CONV, FFN_HIDDEN)), _resident((1, FFN_HIDDEN)),
            _resident(wdn.shape), _resident((1, d)),
        ],
        out_specs=pl.BlockSpec((None, tm, d), row),
        out_shape=jax.ShapeDtypeStruct((b, s, d), F32),
        scratch_shapes=[pltpu.VMEM((tm + SUBLANES, FFN_HIDDEN), F32)],
        compiler_params=pltpu.CompilerParams(
            dimension_semantics=("arbitrary", "arbitrary"), vmem_limit_bytes=VMEM_LIMIT_BIG),
        name="merge_convffn",
    )(x, y_ssd, y_att, gates, wsp, wap, wo, norm2_w.reshape(1, d), wup, ffn_conv_w,
      ffn_conv_b.reshape(1, -1), wdn, norm_f_w.reshape(1, d))


def kernel(x, norm1_w, w_in, b_gate, ssd_conv_w, ssd_conv_b, ssd_dt_bias, ssd_a_log, ssd_d, ssd_norm_w,
           w_ssd_proj, w_att_proj, w_out, norm2_w, w_ffn_up, ffn_conv_w, ffn_conv_b, w_ffn_down, norm_f_w):
    assert norm1_w.shape[0] == 1, "single-layer block"
    sz, xbc, dt, dtT, q, k, v, gates = _in_projection(
        x, norm1_w[0], w_in[0], b_gate[0], ssd_conv_w[0], ssd_conv_b[0], ssd_dt_bias[0])
    y_ssd = _ssd(xbc, sz, dt, dtT, ssd_a_log[0], ssd_d[0], ssd_norm_w[0])
    y_att = _moba(q, k, v)
    return _mixer(x, y_ssd, y_att, gates, w_ssd_proj[0], w_att_proj[0], w_out[0], norm2_w[0], w_ffn_up[0],
                  ffn_conv_w[0], ffn_conv_b[0], w_ffn_down[0], norm_f_w)
```

```python
import functools

import jax
import jax.numpy as jnp
import numpy as np
from jax import lax
from jax.experimental import pallas as pl
from jax.experimental.pallas import tpu as pltpu

D_MODEL = 1024
RMS_EPS = 1e-6
NEG_INF = -1e30

SSD_D_INNER = 2048
SSD_HEAD_DIM = 64
SSD_N_HEADS = 32
SSD_N_GROUPS = 8
SSD_HEADS_PER_GROUP = SSD_N_HEADS // SSD_N_GROUPS
SSD_GROUP_WIDTH = SSD_D_INNER // SSD_N_GROUPS
SSD_D_STATE = 128
SSD_CONV = 4
SSD_CHUNK = 128
SSD_GN = SSD_N_GROUPS * SSD_D_STATE
SSD_CONV_DIM = SSD_D_INNER + 2 * SSD_GN

ATT_HEAD_DIM = 64
ATT_N_HEADS = 16
ATT_WIDTH = ATT_N_HEADS * ATT_HEAD_DIM
MOBA_BLOCK = 256
MOBA_TOPK = 3
ROPE_THETA = 500000.0
ROT_DIM = ATT_HEAD_DIM // 4

FFN_HIDDEN = 2816
FFN_CONV = 3

IN_SIZES = (SSD_D_INNER, SSD_CONV_DIM, SSD_N_HEADS, ATT_WIDTH, ATT_WIDTH, ATT_WIDTH, 2 * D_MODEL)
IN_OFFSETS = tuple(int(v) for v in np.cumsum(IN_SIZES)[:-1])

LANES = 128
SUBLANES = 8
VMEM_LIMIT_BIG = 56 * 1024 * 1024
VMEM_LIMIT_SMALL = 40 * 1024 * 1024

ROW_TILE = 256
COL_CHUNK = 512

BF16 = jnp.bfloat16
F32 = jnp.float32


def _dot(a, b):
    return jnp.dot(a, b, preferred_element_type=F32)


def _dot_nt(a, b):
    return lax.dot_general(a, b, (((1,), (1,)), ((), ())), preferred_element_type=F32)


def _dot_tn(a, b):
    return lax.dot_general(a, b, (((0,), (0,)), ((), ())), preferred_element_type=F32)


def _sigmoid(x):
    return 1.0 / (1.0 + jnp.exp(-x))


def _softplus(x):
    return jnp.maximum(x, 0.0) + jnp.log1p(jnp.exp(-jnp.abs(x)))


def _split3(x):
    hi = x.astype(BF16)
    r = x - hi.astype(F32)
    mid = r.astype(BF16)
    lo = (r - mid.astype(F32)).astype(BF16)
    return hi, mid, lo


def _resident(shape):
    nd = len(shape)
    return pl.BlockSpec(shape, lambda *_: (0,) * nd, pipeline_mode=pl.Buffered(1))


def _inproj_kernel(x_ref, n1_ref, wz_ref, wxbc_ref, wdt_ref, wdtT_ref, wq_ref, wk_ref, wv_ref, wg_ref,
                   bg_ref, cw_ref, cb_ref, dtb_ref, dtbT_ref, cos_ref, sa_ref, sb_ref,
                   sz_ref, xbc_ref, dt_ref, dtT_ref, q_ref, k_ref, v_ref, g_ref, raw_ref):
    tm = x_ref.shape[0]
    halo = SUBLANES

    @pl.when(pl.program_id(1) == 0)
    def _():
        raw_ref[0:halo, :] = jnp.zeros((halo, SSD_CONV_DIM), F32)

    xf = x_ref[...]
    var = jnp.mean(xf * xf, axis=-1, keepdims=True)
    hn = (xf * lax.rsqrt(var + RMS_EPS) * n1_ref[...]).astype(BF16)

    for c in range(0, SSD_D_INNER, COL_CHUNK):
        acc = _dot(hn, wz_ref[:, c:c + COL_CHUNK])
        sz_ref[:, c:c + COL_CHUNK] = (acc * _sigmoid(acc)).astype(BF16)

    for c in range(0, SSD_CONV_DIM, COL_CHUNK):
        cs = slice(c, c + COL_CHUNK)
        raw_ref[halo:halo + tm, cs] = _dot(hn, wxbc_ref[:, cs])
        conv = cb_ref[:, cs]
        for j in range(SSD_CONV):
            off = halo - (SSD_CONV - 1) + j
            conv = conv + cw_ref[j:j + 1, cs] * raw_ref[off:off + tm, cs]
        xbc_ref[:, cs] = (conv * _sigmoid(conv)).astype(BF16)
        raw_ref[0:halo, cs] = raw_ref[tm:tm + halo, cs]

    dt_ref[...] = _softplus(_dot(hn, wdt_ref[...]) + dtb_ref[...])
    dtT_ref[...] = _softplus(_dot_nt(wdtT_ref[...], hn) + dtbT_ref[...])

    reps = COL_CHUNK // LANES
    cos = jnp.concatenate([cos_ref[...]] * reps, axis=1)
    sa = jnp.concatenate([sa_ref[...]] * reps, axis=1)
    sb = jnp.concatenate([sb_ref[...]] * reps, axis=1)
    half = ROT_DIM // 2
    for w_ref, o_ref in ((wq_ref, q_ref), (wk_ref, k_ref)):
        for c in range(0, ATT_WIDTH, COL_CHUNK):
            acc = _dot(hn, w_ref[:, c:c + COL_CHUNK])
            nxt = pltpu.roll(acc, COL_CHUNK - half, axis=1)
            prv = pltpu.roll(acc, half, axis=1)
            o_ref[:, c:c + COL_CHUNK] = (acc * cos + nxt * sa + prv * sb).astype(BF16)
    for c in range(0, ATT_WIDTH, COL_CHUNK):
        v_ref[:, c:c + COL_CHUNK] = _dot(hn, wv_ref[:, c:c + COL_CHUNK]).astype(BF16)

    for c in range(0, 2 * D_MODEL, COL_CHUNK):
        acc = _dot(hn, wg_ref[:, c:c + COL_CHUNK]) + bg_ref[:, c:c + COL_CHUNK]
        g_ref[:, c:c + COL_CHUNK] = _sigmoid(acc).astype(BF16)


def _rope_tables(s):
    half = ROT_DIM // 2
    pos = jnp.arange(s, dtype=F32)
    inv_freq = jnp.float32(ROPE_THETA) ** (-jnp.arange(0, ROT_DIM, 2, dtype=F32) / ROT_DIM)
    ang = pos[:, None] * inv_freq[None, :]
    ang = jnp.concatenate([ang, ang], axis=-1)
    cos, sin = jnp.cos(ang), jnp.sin(ang)
    pad = ATT_HEAD_DIM - ROT_DIM
    cos_t = jnp.concatenate([cos, jnp.ones((s, pad), F32)], axis=-1)
    sa_t = jnp.concatenate([-sin[:, :half], jnp.zeros((s, ATT_HEAD_DIM - half), F32)], axis=-1)
    sb_t = jnp.concatenate([jnp.zeros((s, half), F32), sin[:, half:], jnp.zeros((s, pad), F32)], axis=-1)
    reps = LANES // ATT_HEAD_DIM
    return tuple(jnp.tile(t, (1, reps)) for t in (cos_t, sa_t, sb_t))


def _in_projection(x, norm1_w, w_in, b_gate, conv_w, conv_b, dt_bias):
    b, s, d = x.shape
    tm = ROW_TILE
    wz, wxbc, wdt, wq, wk, wv, wg = (w.astype(BF16) for w in jnp.split(w_in, IN_OFFSETS, axis=-1))
    cos_t, sa_t, sb_t = _rope_tables(s)
    row = lambda bi, ti: (bi, ti, 0)
    tab = lambda bi, ti: (ti, 0)

    def out(width, dtype=BF16):
        return jax.ShapeDtypeStruct((b, s, width), dtype)

    in_specs = [
        pl.BlockSpec((None, tm, d), row),
        _resident((1, d)),
        _resident(wz.shape), _resident(wxbc.shape), _resident(wdt.shape), _resident((SSD_N_HEADS, d)),
        _resident(wq.shape), _resident(wk.shape), _resident(wv.shape), _resident(wg.shape),
        _resident((1, 2 * D_MODEL)), _resident((SSD_CONV, SSD_CONV_DIM)), _resident((1, SSD_CONV_DIM)),
        _resident((1, SSD_N_HEADS)), _resident((SSD_N_HEADS, 1)),
        pl.BlockSpec((tm, LANES), tab), pl.BlockSpec((tm, LANES), tab), pl.BlockSpec((tm, LANES), tab),
    ]
    out_specs = [
        pl.BlockSpec((None, tm, SSD_D_INNER), row),
        pl.BlockSpec((None, tm, SSD_CONV_DIM), row),
        pl.BlockSpec((None, tm, SSD_N_HEADS), row),
        pl.BlockSpec((None, SSD_N_HEADS, tm), lambda bi, ti: (bi, 0, ti)),
        pl.BlockSpec((None, tm, ATT_WIDTH), row),
        pl.BlockSpec((None, tm, ATT_WIDTH), row),
        pl.BlockSpec((None, tm, ATT_WIDTH), row),
        pl.BlockSpec((None, tm, 2 * D_MODEL), row),
    ]
    out_shape = [out(SSD_D_INNER), out(SSD_CONV_DIM), out(SSD_N_HEADS, F32),
                 jax.ShapeDtypeStruct((b, SSD_N_HEADS, s), F32),
                 out(ATT_WIDTH), out(ATT_WIDTH), out(ATT_WIDTH), out(2 * D_MODEL)]
    return pl.pallas_call(
        _inproj_kernel,
        grid=(b, s // tm),
        in_specs=in_specs,
        out_specs=out_specs,
        out_shape=out_shape,
        scratch_shapes=[pltpu.VMEM((tm + SUBLANES, SSD_CONV_DIM), F32)],
        compiler_params=pltpu.CompilerParams(
            dimension_semantics=("arbitrary", "arbitrary"), vmem_limit_bytes=VMEM_LIMIT_BIG),
        name="in_projection",
    )(x, norm1_w.reshape(1, d), wz, wxbc, wdt, wdt.T, wq, wk, wv, wg,
      b_gate.reshape(1, -1), conv_w, conv_b.reshape(1, -1),
      dt_bias.reshape(1, -1), dt_bias.reshape(-1, 1), cos_t, sa_t, sb_t)


def _ssd_kernel(xbc_ref, sz_ref, dt_ref, dtT_ref, alog_ref, alogT_ref, dsk_ref, nw_ref, e_ch_ref,
                y_ref, state_ref):
    l = SSD_CHUNK
    gw = SSD_GROUP_WIDTH
    hpg = SSD_HEADS_PER_GROUP
    p = SSD_HEAD_DIM

    @pl.when(pl.program_id(1) == 0)
    def _():
        state_ref[...] = jnp.zeros(state_ref.shape, F32)

    dt = dt_ref[...]
    a = dt * (-jnp.exp(alog_ref[...]))
    aT = dtT_ref[...] * (-jnp.exp(alogT_ref[...]))

    ri = lax.broadcasted_iota(jnp.int32, (l, l), 0)
    ci = lax.broadcasted_iota(jnp.int32, (l, l), 1)
    lower = ri >= ci
    tri = jnp.where(lower, 1.0, 0.0).astype(BF16)
    triT = jnp.where(ri <= ci, 1.0, 0.0).astype(BF16)
    cs = sum(_dot(tri, piece) for piece in _split3(a))
    csT = sum(_dot(piece, triT) for piece in _split3(aT))
    dt_pieces = jnp.concatenate(_split3(dt), axis=1)
    cs_pieces = jnp.concatenate(_split3(cs), axis=1)

    lane_head = lax.broadcasted_iota(jnp.int32, (l, gw), 1) // p

    for g in range(SSD_N_GROUPS):
        xg = xbc_ref[:, g * gw:(g + 1) * gw].astype(F32)
        bg = xbc_ref[:, SSD_D_INNER + g * SSD_D_STATE:SSD_D_INNER + (g + 1) * SSD_D_STATE]
        cg = xbc_ref[:, SSD_D_INNER + SSD_GN + g * SSD_D_STATE:
                     SSD_D_INNER + SSD_GN + (g + 1) * SSD_D_STATE]
        cb = _dot_nt(cg, bg)
        dt_g = _dot(dt_pieces, e_ch_ref[:, g * gw:(g + 1) * gw])
        cs_g = _dot(cs_pieces, e_ch_ref[:, g * gw:(g + 1) * gw])
        tot_g = cs_g[l - 1:l, :]
        xdt = xg * dt_g
        xdec = (xdt * jnp.exp(tot_g - cs_g)).astype(BF16)

        y = xg * dsk_ref[:, g * gw:(g + 1) * gw]
        for r0 in range(0, hpg, 2):
            mix, rhs = [], []
            for r in (r0, r0 + 1):
                h = g * hpg + r
                decay = jnp.exp(cs[:, h:h + 1] - csT[h:h + 1, :])
                mix.append(jnp.where(lower, cb * decay, 0.0).astype(BF16))
                rhs.append(jnp.where(lane_head == r, xdt, 0.0).astype(BF16))
            y = y + _dot(jnp.concatenate(mix, axis=1), jnp.concatenate(rhs, axis=0))

        st = state_ref[g]
        y = y + _dot(cg, st.astype(BF16)) * jnp.exp(cs_g)
        state_ref[g] = st * jnp.exp(tot_g) + _dot_tn(bg, xdec)

        y = y * sz_ref[:, g * gw:(g + 1) * gw].astype(F32)
        ms = jnp.mean(y * y, axis=-1, keepdims=True)
        y = y * lax.rsqrt(ms + RMS_EPS) * nw_ref[:, g * gw:(g + 1) * gw]
        y_ref[:, g * gw:(g + 1) * gw] = y.astype(BF16)


def _ssd(xbc, sz, dt, dtT, a_log, d_skip, norm_w):
    b, s, _ = xbc.shape
    l = SSD_CHUNK
    row = lambda bi, ci: (bi, ci, 0)
    piece_head = np.arange(3 * SSD_N_HEADS)[:, None] % SSD_N_HEADS
    e_ch = jnp.asarray(piece_head == np.arange(SSD_D_INNER)[None, :] // SSD_HEAD_DIM, BF16)
    return pl.pallas_call(
        _ssd_kernel,
        grid=(b, s // l),
        in_specs=[
            pl.BlockSpec((None, l, SSD_CONV_DIM), row),
            pl.BlockSpec((None, l, SSD_D_INNER), row),
            pl.BlockSpec((None, l, SSD_N_HEADS), row),
            pl.BlockSpec((None, SSD_N_HEADS, l), lambda bi, ci: (bi, 0, ci)),
            _resident((1, SSD_N_HEADS)), _resident((SSD_N_HEADS, 1)),
            _resident((1, SSD_D_INNER)), _resident((1, SSD_D_INNER)),
            _resident(e_ch.shape),
        ],
        out_specs=pl.BlockSpec((None, l, SSD_D_INNER), row),
        out_shape=jax.ShapeDtypeStruct((b, s, SSD_D_INNER), BF16),
        scratch_shapes=[pltpu.VMEM((SSD_N_GROUPS, SSD_D_STATE, SSD_GROUP_WIDTH), F32)],
        compiler_params=pltpu.CompilerParams(
            dimension_semantics=("arbitrary", "arbitrary"), vmem_limit_bytes=VMEM_LIMIT_SMALL),
        name="ssd_scan",
    )(xbc, sz, dt, dtT, a_log.reshape(1, -1), a_log.reshape(-1, 1),
      jnp.repeat(d_skip, SSD_HEAD_DIM).reshape(1, -1), norm_w.reshape(1, -1), e_ch)


VT_ROWS = ATT_HEAD_DIM + 16


def _moba_kernel(q_ref, k_ref, v_ref, o_ref, vt_ref, s_ref, p_ref):
    s = k_ref.shape[0]
    blk = MOBA_BLOCK
    nb = s // blk
    nbp = -(-nb // SUBLANES) * SUBLANES
    hd = ATT_HEAD_DIM

    for n in range(nb):
        vt = v_ref[n * blk:(n + 1) * blk, :].astype(F32).T
        for h in range(2):
            vt_ref[h, 0:hd, n * blk:(n + 1) * blk] = vt[h * hd:(h + 1) * hd].astype(BF16)
    vt_ref[:, hd:VT_ROWS, :] = jnp.ones((2, VT_ROWS - hd, s), BF16)

    arow = lax.broadcasted_iota(jnp.int32, (nbp, s), 0)
    acol = lax.broadcasted_iota(jnp.int32, (nbp, s), 1) // blk
    avg = jnp.where(arow == acol, 1.0 / blk, 0.0).astype(BF16)
    km = _dot(avg, k_ref[...])
    km_hi = km.astype(BF16)
    km_lo = (km - km_hi.astype(F32)).astype(BF16)

    head_row = lax.broadcasted_iota(jnp.int32, (LANES, blk), 0) // hd
    n_idx = lax.broadcasted_iota(jnp.int32, (nbp, blk), 0)
    causal_t = (lax.broadcasted_iota(jnp.int32, (blk, blk), 0) <=
                lax.broadcasted_iota(jnp.int32, (blk, blk), 1))
    scale = hd ** -0.5

    n_slots = s_ref.shape[0]
    units = [(qi, h) for qi in range(nb) for h in range(2)]
    state = {}

    def score_steps(u):
        qi, h = units[u]
        slot = u % n_slots
        st = state[u] = {"biases": [], "m": None}

        def prologue():
            if h == 0:
                st["qt"] = q_ref[qi * blk:(qi + 1) * blk, :].astype(F32).T * scale
            else:
                st["qt"] = state[u - 1]["qt"]
            qt_h = st["qt_h"] = jnp.where(head_row == h, st["qt"], 0.0).astype(BF16)
            if qi > 0:
                gt = _dot(km_hi, qt_h) + _dot(km_lo, qt_h)
                rank = jnp.zeros((nbp, blk), jnp.int32)
                for m in range(qi):
                    gm = gt[m:m + 1, :]
                    beats = (gm > gt) | ((gm == gt) & (n_idx > m))
                    rank = rank + jnp.where(beats, 1, 0)
                st["bias_t"] = jnp.where(rank >= MOBA_TOPK, NEG_INF, 0.0)

        def block(n):
            rows = slice(n * blk, (n + 1) * blk)
            sc = _dot(k_ref[rows, :], st["qt_h"])
            if n == qi:
                sc = jnp.where(causal_t, sc, NEG_INF)
                b_n = jnp.zeros((1, blk), F32)
            else:
                b_n = st["bias_t"][n:n + 1, :]
            s_ref[slot, rows, :] = sc
            m_n = jnp.max(sc, axis=0, keepdims=True) + b_n
            st["m"] = m_n if st["m"] is None else jnp.maximum(st["m"], m_n)
            st["biases"].append(b_n)

        return [prologue] + [functools.partial(block, n) for n in range(qi + 1)]

    def prob_steps(u):
        qi, _ = units[u]
        slot = u % n_slots
        st = state[u]

        def block(n):
            rows = slice(n * blk, (n + 1) * blk)
            p_ref[slot, rows, :] = jnp.exp(s_ref[slot, rows, :] - (st["m"] - st["biases"][n])).astype(BF16)

        return [functools.partial(block, n) for n in range(qi + 1)]

    def finish(u):
        qi, h = units[u]
        keys = (qi + 1) * blk
        acc = _dot(vt_ref[h, :, 0:keys], p_ref[u % n_slots, 0:keys, :])
        state[u]["out"] = acc[0:hd] * (1.0 / acc[hd:hd + 1])
        if h == 1:
            pair = jnp.concatenate([state[u - 1]["out"], state[u]["out"]], axis=0)
            o_ref[qi * blk:(qi + 1) * blk, :] = pair.T.astype(BF16)
            del state[u - 1]

    for step in score_steps(0):
        step()
    for u in range(len(units)):
        ahead = score_steps(u + 1) if u + 1 < len(units) else []
        behind = prob_steps(u)
        for i in range(max(len(ahead), len(behind))):
            if i < len(ahead):
                ahead[i]()
            if i < len(behind):
                behind[i]()
        finish(u)


def _moba(q, k, v):
    b, s, _ = q.shape
    spec = pl.BlockSpec((None, s, LANES), lambda bi, hi: (bi, 0, hi))
    return pl.pallas_call(
        _moba_kernel,
        grid=(b, ATT_N_HEADS // 2),
        in_specs=[spec, spec, spec],
        out_specs=spec,
        out_shape=jax.ShapeDtypeStruct((b, s, ATT_WIDTH), BF16),
        scratch_shapes=[pltpu.VMEM((2, VT_ROWS, s), BF16), pltpu.VMEM((4, s, MOBA_BLOCK), F32),
                        pltpu.VMEM((4, s, MOBA_BLOCK), BF16)],
        compiler_params=pltpu.CompilerParams(
            dimension_semantics=("arbitrary", "arbitrary"), vmem_limit_bytes=VMEM_LIMIT_SMALL),
        name="moba_attention",
    )(q, k, v)


def _mixer_kernel(x_ref, ys_ref, ya_ref, g_ref, wsp_ref, wap_ref, wo_ref, n2_ref, wup_ref, fcw_ref, fcb_ref,
                  wdn_ref, nf_ref, o_ref, up_ref):
    tm = x_ref.shape[0]
    halo = SUBLANES

    @pl.when(pl.program_id(1) == 0)
    def _():
        up_ref[0:halo, :] = jnp.zeros((halo, FFN_HIDDEN), F32)

    merged = (g_ref[:, :D_MODEL].astype(F32) * _dot(ys_ref[...], wsp_ref[...]) +
              g_ref[:, D_MODEL:].astype(F32) * _dot(ya_ref[...], wap_ref[...]))
    x1 = x_ref[...] + _dot(merged.astype(BF16), wo_ref[...])

    var = jnp.mean(x1 * x1, axis=-1, keepdims=True)
    h2 = (x1 * lax.rsqrt(var + RMS_EPS) * n2_ref[...]).astype(BF16)

    n_chunk = FFN_HIDDEN // 2
    y = x1
    for c in range(0, FFN_HIDDEN, n_chunk):
        cs = slice(c, c + n_chunk)
        up_ref[halo:halo + tm, cs] = _dot(h2, wup_ref[:, cs])
        conv = fcb_ref[:, cs]
        for j in range(FFN_CONV):
            off = halo - (FFN_CONV - 1) + j
            conv = conv + fcw_ref[j:j + 1, cs] * up_ref[off:off + tm, cs]
        gate = _dot(h2, wup_ref[:, FFN_HIDDEN + c:FFN_HIDDEN + c + n_chunk])
        act = (conv * _sigmoid(conv) * gate).astype(BF16)
        up_ref[0:halo, cs] = up_ref[tm:tm + halo, cs]
        y = y + _dot(act, wdn_ref[cs, :])

    var = jnp.mean(y * y, axis=-1, keepdims=True)
    o_ref[...] = y * lax.rsqrt(var + RMS_EPS) * nf_ref[...]


def _mixer(x, y_ssd, y_att, gates, w_ssd_proj, w_att_proj, w_out, norm2_w, w_ffn_up, ffn_conv_w, ffn_conv_b,
           w_ffn_down, norm_f_w):
    b, s, d = x.shape
    tm = ROW_TILE
    row = lambda bi, ti: (bi, ti, 0)
    wsp, wap, wo, wup, wdn = (w.astype(BF16) for w in (w_ssd_proj, w_att_proj, w_out, w_ffn_up, w_ffn_down))
    return pl.pallas_call(
        _mixer_kernel,
        grid=(b, s // tm),
        in_specs=[
            pl.BlockSpec((None, tm, d), row),
            pl.BlockSpec((None, tm, SSD_D_INNER), row),
            pl.BlockSpec((None, tm, ATT_WIDTH), row),
            pl.BlockSpec((None, tm, 2 * D_MODEL), row),
            _resident(wsp.shape), _resident(wap.shape), _resident(wo.shape), _resident((1, d)),
            _resident(wup.shape), _resident((FFN_CONV, FFN_HIDDEN)), _resident((1, FFN_HIDDEN)),
            _resident(wdn.shape), _resident((1, d)),
        ],
        out_specs=pl.BlockSpec((None, tm, d), row),
        out_shape=jax.ShapeDtypeStruct((b, s, d), F32),
        scratch_shapes=[pltpu.VMEM((tm + SUBLANES, FFN_HIDDEN), F32)],
        compiler_params=pltpu.CompilerParams(
            dimension_semantics=("arbitrary", "arbitrary"), vmem_limit_bytes=VMEM_LIMIT_BIG),
        name="merge_convffn",
    )(x, y_ssd, y_att, gates, wsp, wap, wo, norm2_w.reshape(1, d), wup, ffn_conv_w,
      ffn_conv_b.reshape(1, -1), wdn, norm_f_w.reshape(1, d))


def kernel(x, norm1_w, w_in, b_gate, ssd_conv_w, ssd_conv_b, ssd_dt_bias, ssd_a_log, ssd_d, ssd_norm_w,
           w_ssd_proj, w_att_proj, w_out, norm2_w, w_ffn_up, ffn_conv_w, ffn_conv_b, w_ffn_down, norm_f_w):
    assert norm1_w.shape[0] == 1, "single-layer block"
    sz, xbc, dt, dtT, q, k, v, gates = _in_projection(
        x, norm1_w[0], w_in[0], b_gate[0], ssd_conv_w[0], ssd_conv_b[0], ssd_dt_bias[0])
    y_ssd = _ssd(xbc, sz, dt, dtT, ssd_a_log[0], ssd_d[0], ssd_norm_w[0])
    y_att = _moba(q, k, v)
    return _mixer(x, y_ssd, y_att, gates, w_ssd_proj[0], w_att_proj[0], w_out[0], norm2_w[0], w_ffn_up[0],
                  ffn_conv_w[0], ffn_conv_b[0], w_ffn_down[0], norm_f_w)
```

```python
import functools

import jax
import jax.numpy as jnp
import numpy as np
from jax import lax
from jax.experimental import pallas as pl
from jax.experimental.pallas import tpu as pltpu

D_MODEL = 1024
RMS_EPS = 1e-6
NEG_INF = -1e30

SSD_D_INNER = 2048
SSD_HEAD_DIM = 64
SSD_N_HEADS = 32
SSD_N_GROUPS = 8
SSD_HEADS_PER_GROUP = SSD_N_HEADS // SSD_N_GROUPS
SSD_GROUP_WIDTH = SSD_D_INNER // SSD_N_GROUPS
SSD_D_STATE = 128
SSD_CONV = 4
SSD_CHUNK = 128
SSD_GN = SSD_N_GROUPS * SSD_D_STATE
SSD_CONV_DIM = SSD_D_INNER + 2 * SSD_GN

ATT_HEAD_DIM = 64
ATT_N_HEADS = 16
ATT_WIDTH = ATT_N_HEADS * ATT_HEAD_DIM
MOBA_BLOCK = 256
MOBA_TOPK = 3
ROPE_THETA = 500000.0
ROT_DIM = ATT_HEAD_DIM // 4

FFN_HIDDEN = 2816
FFN_CONV = 3

IN_SIZES = (SSD_D_INNER, SSD_CONV_DIM, SSD_N_HEADS, ATT_WIDTH, ATT_WIDTH, ATT_WIDTH, 2 * D_MODEL)
IN_OFFSETS = tuple(int(v) for v in np.cumsum(IN_SIZES)[:-1])

LANES = 128
SUBLANES = 8
VMEM_LIMIT_BIG = 56 * 1024 * 1024
VMEM_LIMIT_SMALL = 40 * 1024 * 1024

ROW_TILE = 256
COL_CHUNK = 512

BF16 = jnp.bfloat16
F32 = jnp.float32


def _dot(a, b):
    return jnp.dot(a, b, preferred_element_type=F32)


def _dot_nt(a, b):
    return lax.dot_general(a, b, (((1,), (1,)), ((), ())), preferred_element_type=F32)


def _dot_tn(a, b):
    return lax.dot_general(a, b, (((0,), (0,)), ((), ())), preferred_element_type=F32)


def _sigmoid(x):
    return 1.0 / (1.0 + jnp.exp2(x * float(-np.log2(np.e))))


def _softplus(x):
    return jnp.maximum(x, 0.0) + jnp.log1p(jnp.exp(-jnp.abs(x)))


def _split3(x):
    hi = x.astype(BF16)
    r = x - hi.astype(F32)
    mid = r.astype(BF16)
    lo = (r - mid.astype(F32)).astype(BF16)
    return hi, mid, lo


def _resident(shape):
    nd = len(shape)
    return pl.BlockSpec(shape, lambda *_: (0,) * nd, pipeline_mode=pl.Buffered(1))


def _inproj_kernel(x_ref, n1_ref, wz_ref, wxbc_ref, wdt_ref, wdtT_ref, wq_ref, wk_ref, wv_ref, wg_ref,
                   cw_ref, cb_ref, dtb_ref, dtbT_ref, cos_ref, sa_ref, sb_ref,
                   sz_ref, xbc_ref, dt_ref, dtT_ref, q_ref, k_ref, v_ref, g_ref, raw_ref):
    tm = x_ref.shape[0]
    halo = SUBLANES

    @pl.when(pl.program_id(1) == 0)
    def _():
        raw_ref[0:halo, :] = jnp.zeros((halo, SSD_CONV_DIM), F32)

    xf = x_ref[...]
    var = jnp.mean(xf * xf, axis=-1, keepdims=True)
    hn = (xf * lax.rsqrt(var + RMS_EPS) * n1_ref[...]).astype(BF16)

    for c in range(0, SSD_D_INNER, COL_CHUNK):
        acc = _dot(hn, wz_ref[:, c:c + COL_CHUNK])
        sz_ref[:, c:c + COL_CHUNK] = (acc * _sigmoid(acc)).astype(BF16)

    for c in range(0, SSD_CONV_DIM, COL_CHUNK):
        cs = slice(c, c + COL_CHUNK)
        raw_ref[halo:halo + tm, cs] = _dot(hn, wxbc_ref[:, cs])
        conv = cb_ref[:, cs]
        for j in range(SSD_CONV):
            off = halo - (SSD_CONV - 1) + j
            conv = conv + cw_ref[j:j + 1, cs] * raw_ref[off:off + tm, cs]
        xbc_ref[:, cs] = (conv * _sigmoid(conv)).astype(BF16)
        raw_ref[0:halo, cs] = raw_ref[tm:tm + halo, cs]

    dt_ref[...] = _softplus(_dot(hn, wdt_ref[...]) + dtb_ref[...])
    dtT_ref[...] = _softplus(_dot_nt(wdtT_ref[...], hn) + dtbT_ref[...])

    cos, sa, sb = cos_ref[...], sa_ref[...], sb_ref[...]
    half = ROT_DIM // 2
    for w_ref, o_ref in ((wq_ref, q_ref), (wk_ref, k_ref)):
        for c in range(0, ATT_WIDTH, COL_CHUNK):
            acc = _dot(hn, w_ref[:, c:c + COL_CHUNK])
            for j in range(0, COL_CHUNK, LANES):
                t = acc[:, j:j + LANES]
                nxt = pltpu.roll(t, LANES - half, axis=1)
                prv = pltpu.roll(t, half, axis=1)
                o_ref[:, c + j:c + j + LANES] = (t * cos + nxt * sa + prv * sb).astype(BF16)
    for c in range(0, ATT_WIDTH, COL_CHUNK):
        v_ref[:, c:c + COL_CHUNK] = _dot(hn, wv_ref[:, c:c + COL_CHUNK]).astype(BF16)

    for c in range(0, 2 * D_MODEL, COL_CHUNK):
        g_ref[:, c:c + COL_CHUNK] = _dot(hn, wg_ref[:, c:c + COL_CHUNK]).astype(BF16)


def _rope_tables(s):
    half = ROT_DIM // 2
    pos = jnp.arange(s, dtype=F32)
    inv_freq = jnp.float32(ROPE_THETA) ** (-jnp.arange(0, ROT_DIM, 2, dtype=F32) / ROT_DIM)
    ang = pos[:, None] * inv_freq[None, :]
    ang = jnp.concatenate([ang, ang], axis=-1)
    cos, sin = jnp.cos(ang), jnp.sin(ang)
    pad = ATT_HEAD_DIM - ROT_DIM
    cos_t = jnp.concatenate([cos, jnp.ones((s, pad), F32)], axis=-1)
    sa_t = jnp.concatenate([-sin[:, :half], jnp.zeros((s, ATT_HEAD_DIM - half), F32)], axis=-1)
    sb_t = jnp.concatenate([jnp.zeros((s, half), F32), sin[:, half:], jnp.zeros((s, pad), F32)], axis=-1)
    reps = LANES // ATT_HEAD_DIM
    return tuple(jnp.tile(t, (1, reps)) for t in (cos_t, sa_t, sb_t))


def _in_projection(x, norm1_w, w_in, conv_w, conv_b, dt_bias):
    b, s, d = x.shape
    tm = ROW_TILE
    wz, wxbc, wdt, wq, wk, wv, wg = jnp.split(w_in.astype(BF16), IN_OFFSETS, axis=-1)
    cos_t, sa_t, sb_t = _rope_tables(s)
    row = lambda bi, ti: (bi, ti, 0)
    tab = lambda bi, ti: (ti, 0)

    def out(width, dtype=BF16):
        return jax.ShapeDtypeStruct((b, s, width), dtype)

    in_specs = [
        pl.BlockSpec((None, tm, d), row),
        _resident((1, d)),
        _resident(wz.shape), _resident(wxbc.shape), _resident(wdt.shape), _resident((SSD_N_HEADS, d)),
        _resident(wq.shape), _resident(wk.shape), _resident(wv.shape), _resident(wg.shape),
        _resident((SSD_CONV, SSD_CONV_DIM)), _resident((1, SSD_CONV_DIM)),
        _resident((1, SSD_N_HEADS)), _resident((SSD_N_HEADS, 1)),
        pl.BlockSpec((tm, LANES), tab), pl.BlockSpec((tm, LANES), tab), pl.BlockSpec((tm, LANES), tab),
    ]
    out_specs = [
        pl.BlockSpec((None, tm, SSD_D_INNER), row),
        pl.BlockSpec((None, tm, SSD_CONV_DIM), row),
        pl.BlockSpec((None, tm, SSD_N_HEADS), row),
        pl.BlockSpec((None, SSD_N_HEADS, tm), lambda bi, ti: (bi, 0, ti)),
        pl.BlockSpec((None, tm, ATT_WIDTH), row),
        pl.BlockSpec((None, tm, ATT_WIDTH), row),
        pl.BlockSpec((None, tm, ATT_WIDTH), row),
        pl.BlockSpec((None, tm, 2 * D_MODEL), row),
    ]
    out_shape = [out(SSD_D_INNER), out(SSD_CONV_DIM), out(SSD_N_HEADS, F32),
                 jax.ShapeDtypeStruct((b, SSD_N_HEADS, s), F32),
                 out(ATT_WIDTH), out(ATT_WIDTH), out(ATT_WIDTH), out(2 * D_MODEL)]
    return pl.pallas_call(
        _inproj_kernel,
        grid=(b, s // tm),
        in_specs=in_specs,
        out_specs=out_specs,
        out_shape=out_shape,
        scratch_shapes=[pltpu.VMEM((tm + SUBLANES, SSD_CONV_DIM), F32)],
        compiler_params=pltpu.CompilerParams(
            dimension_semantics=("arbitrary", "arbitrary"), vmem_limit_bytes=VMEM_LIMIT_BIG),
        name="in_projection",
    )(x, norm1_w.reshape(1, d), wz, wxbc, wdt, wdt.T, wq, wk, wv, wg,
      conv_w, conv_b.reshape(1, -1),
      dt_bias.reshape(1, -1), dt_bias.reshape(-1, 1), cos_t, sa_t, sb_t)


def _ssd_kernel(xbc_ref, sz_ref, dt_ref, dtT_ref, alog_ref, alogT_ref, dsk_ref, nw_ref, e_ch_ref,
                y_ref, state_ref):
    l = SSD_CHUNK
    gw = SSD_GROUP_WIDTH
    hpg = SSD_HEADS_PER_GROUP
    p = SSD_HEAD_DIM

    @pl.when(pl.program_id(1) == 0)
    def _():
        state_ref[...] = jnp.zeros(state_ref.shape, F32)

    log2e = float(np.log2(np.e))
    dt = dt_ref[...]
    a = dt * (-log2e * jnp.exp(alog_ref[...]))
    aT = dtT_ref[...] * (-log2e * jnp.exp(alogT_ref[...]))

    ri = lax.broadcasted_iota(jnp.int32, (l, l), 0)
    ci = lax.broadcasted_iota(jnp.int32, (l, l), 1)
    lower = ri >= ci
    tri = jnp.where(lower, 1.0, 0.0).astype(BF16)
    triT = jnp.where(ri <= ci, 1.0, 0.0).astype(BF16)
    cs = sum(_dot(tri, piece) for piece in _split3(a))
    csT = sum(_dot(piece, triT) for piece in _split3(aT))
    dt_pieces = jnp.concatenate(_split3(dt), axis=1)
    cs_pieces = jnp.concatenate(_split3(cs), axis=1)

    lane_head = lax.broadcasted_iota(jnp.int32, (l, gw), 1) // p

    def group_steps(g):
        cols = slice(g * gw, (g + 1) * gw)
        v = {}

        def load_and_project():
            v["xg"] = xbc_ref[:, cols].astype(F32)
            v["bg"] = xbc_ref[:, SSD_D_INNER + g * SSD_D_STATE:SSD_D_INNER + (g + 1) * SSD_D_STATE]
            v["cg"] = xbc_ref[:, SSD_D_INNER + SSD_GN + g * SSD_D_STATE:
                              SSD_D_INNER + SSD_GN + (g + 1) * SSD_D_STATE]
            v["cb"] = _dot_nt(v["cg"], v["bg"])
            v["dt_g"] = _dot(dt_pieces, e_ch_ref[:, cols])
            v["cs_g"] = _dot(cs_pieces, e_ch_ref[:, cols])
            v["st"] = state_ref[g]
            v["y_off"] = _dot(v["cg"], v["st"].astype(BF16))

        def scale_inputs():
            tot_g = v["tot_g"] = v["cs_g"][l - 1:l, :]
            v["xdt"] = v["xg"] * v["dt_g"]
            v["xdec"] = (v["xdt"] * jnp.exp2(tot_g - v["cs_g"])).astype(BF16)
            v["y"] = v["xg"] * dsk_ref[:, cols] + v["y_off"] * jnp.exp2(v["cs_g"])

        def head_pair(r0):
            mix, rhs = [], []
            for r in (r0, r0 + 1):
                h = g * hpg + r
                decay = jnp.exp2(cs[:, h:h + 1] - csT[h:h + 1, :])
                mix.append(jnp.where(lower, v["cb"] * decay, 0.0).astype(BF16))
                rhs.append(jnp.where(lane_head == r, v["xdt"], 0.0).astype(BF16))
            v["y"] = v["y"] + _dot(jnp.concatenate(mix, axis=1), jnp.concatenate(rhs, axis=0))

        def update_state():
            state_ref[g] = v["st"] * jnp.exp2(v["tot_g"]) + _dot_tn(v["bg"], v["xdec"])

        def gate_and_norm():
            y = v["y"] * sz_ref[:, cols].astype(F32)
            ms = jnp.mean(y * y, axis=-1, keepdims=True)
            y_ref[:, cols] = (y * lax.rsqrt(ms + RMS_EPS) * nw_ref[:, cols]).astype(BF16)

        return ([load_and_project, scale_inputs] +
                [functools.partial(head_pair, r0) for r0 in range(0, hpg, 2)] +
                [update_state, gate_and_norm])

    pending = [group_steps(g) for g in range(SSD_N_GROUPS)]
    n_steps = len(pending[0])
    for t in range(SSD_N_GROUPS + n_steps - 1):
        for g in range(SSD_N_GROUPS):
            if 0 <= t - g < n_steps:
                pending[g][t - g]()


def _ssd(xbc, sz, dt, dtT, a_log, d_skip, norm_w):
    b, s, _ = xbc.shape
    l = SSD_CHUNK
    row = lambda bi, ci: (bi, ci, 0)
    piece_head = np.arange(3 * SSD_N_HEADS)[:, None] % SSD_N_HEADS
    e_ch = jnp.asarray(piece_head == np.arange(SSD_D_INNER)[None, :] // SSD_HEAD_DIM, BF16)
    return pl.pallas_call(
        _ssd_kernel,
        grid=(b, s // l),
        in_specs=[
            pl.BlockSpec((None, l, SSD_CONV_DIM), row),
            pl.BlockSpec((None, l, SSD_D_INNER), row),
            pl.BlockSpec((None, l, SSD_N_HEADS), row),
            pl.BlockSpec((None, SSD_N_HEADS, l), lambda bi, ci: (bi, 0, ci)),
            _resident((1, SSD_N_HEADS)), _resident((SSD_N_HEADS, 1)),
            _resident((1, SSD_D_INNER)), _resident((1, SSD_D_INNER)),
            _resident(e_ch.shape),
        ],
        out_specs=pl.BlockSpec((None, l, SSD_D_INNER), row),
        out_shape=jax.ShapeDtypeStruct((b, s, SSD_D_INNER), BF16),
        scratch_shapes=[pltpu.VMEM((SSD_N_GROUPS, SSD_D_STATE, SSD_GROUP_WIDTH), F32)],
        compiler_params=pltpu.CompilerParams(
            dimension_semantics=("arbitrary", "arbitrary"), vmem_limit_bytes=VMEM_LIMIT_SMALL),
        name="ssd_scan",
    )(xbc, sz, dt, dtT, a_log.reshape(1, -1), a_log.reshape(-1, 1),
      jnp.repeat(d_skip, SSD_HEAD_DIM).reshape(1, -1), norm_w.reshape(1, -1), e_ch)


VT_ROWS = ATT_HEAD_DIM + 16


def _moba_kernel(q_ref, k_ref, v_ref, o_ref, vt_ref, s_ref, p_ref):
    s = k_ref.shape[0]
    blk = MOBA_BLOCK
    nb = s // blk
    nbp = -(-nb // SUBLANES) * SUBLANES
    hd = ATT_HEAD_DIM

    for n in range(nb):
        vt = v_ref[n * blk:(n + 1) * blk, :].astype(F32).T
        for h in range(2):
            vt_ref[h, 0:hd, n * blk:(n + 1) * blk] = vt[h * hd:(h + 1) * hd].astype(BF16)
    vt_ref[:, hd:VT_ROWS, :] = jnp.ones((2, VT_ROWS - hd, s), BF16)

    arow = lax.broadcasted_iota(jnp.int32, (nbp, s), 0)
    acol = lax.broadcasted_iota(jnp.int32, (nbp, s), 1) // blk
    avg = jnp.where(arow == acol, 1.0 / blk, 0.0).astype(BF16)
    km = _dot(avg, k_ref[...])
    km_hi = km.astype(BF16)
    km_lo = (km - km_hi.astype(F32)).astype(BF16)

    head_row = lax.broadcasted_iota(jnp.int32, (LANES, blk), 0) // hd
    n_idx = lax.broadcasted_iota(jnp.int32, (nbp, blk), 0)
    causal_t = (lax.broadcasted_iota(jnp.int32, (blk, blk), 0) <=
                lax.broadcasted_iota(jnp.int32, (blk, blk), 1))
    scale = hd ** -0.5 * float(np.log2(np.e))

    n_slots = s_ref.shape[0]
    units = [(qi, h) for qi in range(nb) for h in range(2)]
    state = {}

    def score_steps(u):
        qi, h = units[u]
        slot = u % n_slots
        st = state[u] = {"biases": [], "m": None}

        def prologue():
            if h == 0:
                st["qt"] = q_ref[qi * blk:(qi + 1) * blk, :].astype(F32).T * scale
            else:
                st["qt"] = state[u - 1]["qt"]
            qt_h = st["qt_h"] = jnp.where(head_row == h, st["qt"], 0.0).astype(BF16)
            if qi > 0:
                gt = _dot(km_hi, qt_h) + _dot(km_lo, qt_h)
                rank = jnp.zeros((nbp, blk), jnp.int32)
                for m in range(qi):
                    gm = gt[m:m + 1, :]
                    beats = (gm > gt) | ((gm == gt) & (n_idx > m))
                    rank = rank + jnp.where(beats, 1, 0)
                st["bias_t"] = jnp.where(rank >= MOBA_TOPK, NEG_INF, 0.0)

        def block(n):
            rows = slice(n * blk, (n + 1) * blk)
            sc = _dot(k_ref[rows, :], st["qt_h"])
            if n == qi:
                sc = jnp.where(causal_t, sc, NEG_INF)
                b_n = jnp.zeros((1, blk), F32)
            else:
                b_n = st["bias_t"][n:n + 1, :]
            s_ref[slot, rows, :] = sc
            m_n = jnp.max(sc, axis=0, keepdims=True) + b_n
            st["m"] = m_n if st["m"] is None else jnp.maximum(st["m"], m_n)
            st["biases"].append(b_n)

        return [prologue] + [functools.partial(block, n) for n in range(qi + 1)]

    def prob_steps(u):
        qi, _ = units[u]
        slot = u % n_slots
        st = state[u]

        def block(n):
            rows = slice(n * blk, (n + 1) * blk)
            p_ref[slot, rows, :] = jnp.exp2(s_ref[slot, rows, :] - (st["m"] - st["biases"][n])).astype(BF16)

        return [functools.partial(block, n) for n in range(qi + 1)]

    def value_steps(u):
        qi, h = units[u]
        slot = u % n_slots
        st = state[u]
        st["acc"] = None

        def block(n):
            rows = slice(n * blk, (n + 1) * blk)
            part = _dot(vt_ref[h, :, rows], p_ref[slot, rows, :])
            st["acc"] = part if st["acc"] is None else st["acc"] + part

        def epilogue():
            acc = st["acc"]
            st["out"] = acc[0:hd] * (1.0 / acc[hd:hd + 1])
            if h == 1:
                pair = jnp.concatenate([state[u - 1]["out"], st["out"]], axis=0)
                o_ref[qi * blk:(qi + 1) * blk, :] = pair.T.astype(BF16)
                del state[u - 1]

        return [functools.partial(block, n) for n in range(qi + 1)] + [epilogue]

    for t in range(len(units) + 2):
        streams = [score_steps(t) if t < len(units) else [],
                   prob_steps(t - 1) if 0 <= t - 1 < len(units) else [],
                   value_steps(t - 2) if 0 <= t - 2 < len(units) else []]
        for i in range(max(len(stream) for stream in streams)):
            for stream in streams:
                if i < len(stream):
                    stream[i]()


def _moba(q, k, v):
    b, s, _ = q.shape
    spec = pl.BlockSpec((None, s, LANES), lambda bi, hi: (bi, 0, hi))
    return pl.pallas_call(
        _moba_kernel,
        grid=(b, ATT_N_HEADS // 2),
        in_specs=[spec, spec, spec],
        out_specs=spec,
        out_shape=jax.ShapeDtypeStruct((b, s, ATT_WIDTH), BF16),
        scratch_shapes=[pltpu.VMEM((2, VT_ROWS, s), BF16), pltpu.VMEM((4, s, MOBA_BLOCK), F32),
                        pltpu.VMEM((4, s, MOBA_BLOCK), BF16)],
        compiler_params=pltpu.CompilerParams(
            dimension_semantics=("arbitrary", "arbitrary"), vmem_limit_bytes=VMEM_LIMIT_SMALL),
        name="moba_attention",
    )(q, k, v)


def _mixer_kernel(x_ref, ys_ref, ya_ref, g_ref, bg_ref, wsp_ref, wap_ref, wo_ref, n2_ref, wup_ref, fcw_ref,
                  fcb_ref, wdn_ref, nf_ref, o_ref, up_ref):
    tm = x_ref.shape[0]
    halo = SUBLANES

    @pl.when(pl.program_id(1) == 0)
    def _():
        up_ref[0:halo, :] = jnp.zeros((halo, FFN_HIDDEN), F32)

    g_ssd = _sigmoid(g_ref[:, :D_MODEL].astype(F32) + bg_ref[:, :D_MODEL])
    g_att = _sigmoid(g_ref[:, D_MODEL:].astype(F32) + bg_ref[:, D_MODEL:])
    merged = g_ssd * _dot(ys_ref[...], wsp_ref[...]) + g_att * _dot(ya_ref[...], wap_ref[...])
    x1 = x_ref[...] + _dot(merged.astype(BF16), wo_ref[...])

    var = jnp.mean(x1 * x1, axis=-1, keepdims=True)
    h2 = (x1 * lax.rsqrt(var + RMS_EPS) * n2_ref[...]).astype(BF16)

    n_chunk = FFN_HIDDEN // 2
    y = x1
    for c in range(0, FFN_HIDDEN, n_chunk):
        cs = slice(c, c + n_chunk)
        up_ref[halo:halo + tm, cs] = _dot(h2, wup_ref[:, cs])
        conv = fcb_ref[:, cs]
        for j in range(FFN_CONV):
            off = halo - (FFN_CONV - 1) + j
            conv = conv + fcw_ref[j:j + 1, cs] * up_ref[off:off + tm, cs]
        gate = _dot(h2, wup_ref[:, FFN_HIDDEN + c:FFN_HIDDEN + c + n_chunk])
        act = (conv * _sigmoid(conv) * gate).astype(BF16)
        up_ref[0:halo, cs] = up_ref[tm:tm + halo, cs]
        y = y + _dot(act, wdn_ref[cs, :])

    var = jnp.mean(y * y, axis=-1, keepdims=True)
    o_ref[...] = y * lax.rsqrt(var + RMS_EPS) * nf_ref[...]


def _mixer(x, y_ssd, y_att, gates, b_gate, w_ssd_proj, w_att_proj, w_out, norm2_w, w_ffn_up, ffn_conv_w, ffn_conv_b,
           w_ffn_down, norm_f_w):
    b, s, d = x.shape
    tm = ROW_TILE
    row = lambda bi, ti: (bi, ti, 0)
    wsp, wap, wo, wup, wdn = (w.astype(BF16) for w in (w_ssd_proj, w_att_proj, w_out, w_ffn_up, w_ffn_down))
    return pl.pallas_call(
        _mixer_kernel,
        grid=(b, s // tm),
        in_specs=[
            pl.BlockSpec((None, tm, d), row),
            pl.BlockSpec((None, tm, SSD_D_INNER), row),
            pl.BlockSpec((None, tm, ATT_WIDTH), row),
            pl.BlockSpec((None, tm, 2 * D_MODEL), row),
            _resident((1, 2 * D_MODEL)),
            _resident(wsp.shape), _resident(wap.shape), _resident(wo.shape), _resident((1, d)),
            _resident(wup.shape), _resident((FFN_CONV, FFN_HIDDEN)), _resident((1, FFN_HIDDEN)),
            _resident(wdn.shape), _resident((1, d)),
        ],
        out_specs=pl.BlockSpec((None, tm, d), row),
        out_shape=jax.ShapeDtypeStruct((b, s, d), F32),
        scratch_shapes=[pltpu.VMEM((tm + SUBLANES, FFN_HIDDEN), F32)],
        compiler_params=pltpu.CompilerParams(
            dimension_semantics=("arbitrary", "arbitrary"), vmem_limit_bytes=VMEM_LIMIT_BIG),
        name="merge_convffn",
    )(x, y_ssd, y_att, gates, b_gate.reshape(1, -1), wsp, wap, wo, norm2_w.reshape(1, d), wup, ffn_conv_w,
      ffn_conv_b.reshape(1, -1), wdn, norm_f_w.reshape(1, d))


def kernel(x, norm1_w, w_in, b_gate, ssd_conv_w, ssd_conv_b, ssd_dt_bias, ssd_a_log, ssd_d, ssd_norm_w,
           w_ssd_proj, w_att_proj, w_out, norm2_w, w_ffn_up, ffn_conv_w, ffn_conv_b, w_ffn_down, norm_f_w):
    assert norm1_w.shape[0] == 1, "single-layer block"
    sz, xbc, dt, dtT, q, k, v, gates = _in_projection(
        x, norm1_w[0], w_in[0], ssd_conv_w[0], ssd_conv_b[0], ssd_dt_bias[0])
    y_ssd = _ssd(xbc, sz, dt, dtT, ssd_a_log[0], ssd_d[0], ssd_norm_w[0])
    y_att = _moba(q, k, v)
    return _mixer(x, y_ssd, y_att, gates, b_gate[0], w_ssd_proj[0], w_att_proj[0], w_out[0], norm2_w[0], w_ffn_up[0],
                  ffn_conv_w[0], ffn_conv_b[0], w_ffn_down[0], norm_f_w)
```

```python
import functools

import jax
import jax.numpy as jnp
import numpy as np
from jax import lax
from jax.experimental import pallas as pl
from jax.experimental.pallas import tpu as pltpu

D_MODEL = 1024
RMS_EPS = 1e-6
NEG_INF = -1e30

SSD_D_INNER = 2048
SSD_HEAD_DIM = 64
SSD_N_HEADS = 32
SSD_N_GROUPS = 8
SSD_HEADS_PER_GROUP = SSD_N_HEADS // SSD_N_GROUPS
SSD_GROUP_WIDTH = SSD_D_INNER // SSD_N_GROUPS
SSD_D_STATE = 128
SSD_CONV = 4
SSD_CHUNK = 128
SSD_GN = SSD_N_GROUPS * SSD_D_STATE
SSD_CONV_DIM = SSD_D_INNER + 2 * SSD_GN

ATT_HEAD_DIM = 64
ATT_N_HEADS = 16
ATT_WIDTH = ATT_N_HEADS * ATT_HEAD_DIM
MOBA_BLOCK = 256
MOBA_TOPK = 3
ROPE_THETA = 500000.0
ROT_DIM = ATT_HEAD_DIM // 4

FFN_HIDDEN = 2816
FFN_CONV = 3

IN_SIZES = (SSD_D_INNER, SSD_CONV_DIM, SSD_N_HEADS, ATT_WIDTH, ATT_WIDTH, ATT_WIDTH, 2 * D_MODEL)
IN_OFFSETS = tuple(int(v) for v in np.cumsum(IN_SIZES)[:-1])

LANES = 128
SUBLANES = 8
VMEM_LIMIT_BIG = 56 * 1024 * 1024
VMEM_LIMIT_SMALL = 40 * 1024 * 1024

ROW_TILE = 256
COL_CHUNK = 512

BF16 = jnp.bfloat16
F32 = jnp.float32


def _dot(a, b):
    return jnp.dot(a, b, preferred_element_type=F32)


def _dot_nt(a, b):
    return lax.dot_general(a, b, (((1,), (1,)), ((), ())), preferred_element_type=F32)


def _dot_tn(a, b):
    return lax.dot_general(a, b, (((0,), (0,)), ((), ())), preferred_element_type=F32)


def _sigmoid(x):
    return 1.0 / (1.0 + jnp.exp2(x * float(-np.log2(np.e))))


def _softplus(x):
    return jnp.maximum(x, 0.0) + jnp.log1p(jnp.exp(-jnp.abs(x)))


def _split3(x):
    hi = x.astype(BF16)
    r = x - hi.astype(F32)
    mid = r.astype(BF16)
    lo = (r - mid.astype(F32)).astype(BF16)
    return hi, mid, lo


def _resident(shape):
    nd = len(shape)
    return pl.BlockSpec(shape, lambda *_: (0,) * nd, pipeline_mode=pl.Buffered(1))


def _inproj_kernel(x_ref, n1_ref, wz_ref, wxbc_ref, wdt_ref, wdtT_ref, wq_ref, wk_ref, wv_ref, wg_ref,
                   cw_ref, cb_ref, dtb_ref, dtbT_ref, cos_ref, sa_ref, sb_ref,
                   sz_ref, xbc_ref, dt_ref, dtT_ref, q_ref, k_ref, v_ref, g_ref, raw_ref, shift_ref, wcopy_ref):
    tm = x_ref.shape[0]
    halo = SUBLANES

    @pl.when(pl.program_id(1) == 0)
    def _():
        raw_ref[...] = jnp.zeros(raw_ref.shape, F32)

    @pl.when((pl.program_id(0) == 0) & (pl.program_id(1) == 0))
    def _():
        wcopy_ref[...] = wxbc_ref[...]

    xf = x_ref[...]
    var = jnp.mean(xf * xf, axis=-1, keepdims=True)
    hn = (xf * lax.rsqrt(var + RMS_EPS) * n1_ref[...]).astype(BF16)

    for c in range(0, SSD_D_INNER, COL_CHUNK):
        acc = _dot(hn, wz_ref[:, c:c + COL_CHUNK])
        sz_ref[:, c:c + COL_CHUNK] = (acc * _sigmoid(acc)).astype(BF16)

    for i, c in enumerate(range(0, SSD_CONV_DIM, COL_CHUNK)):
        cs = slice(c, c + COL_CHUNK)
        acc = _dot(hn, wcopy_ref[:, cs])
        prev = raw_ref[:, cs]
        conv = cb_ref[:, cs] + cw_ref[SSD_CONV - 1:SSD_CONV, cs] * acc
        for k in range(1, SSD_CONV):
            buf = shift_ref.at[i % 2, k - 1]
            buf[halo:halo + k, :] = prev[halo - k:halo, :]
            buf[halo + k:halo + k + tm, :] = acc
            conv = conv + cw_ref[SSD_CONV - 1 - k:SSD_CONV - k, cs] * buf[halo:halo + tm, :]
        xbc_ref[:, cs] = (conv * _sigmoid(conv)).astype(BF16)
        raw_ref[:, cs] = acc[tm - halo:tm, :]

    dt_ref[...] = _softplus(_dot(hn, wdt_ref[...]) + dtb_ref[...])
    dtT_ref[...] = _softplus(_dot_nt(wdtT_ref[...], hn) + dtbT_ref[...])

    cos, sa, sb = cos_ref[...], sa_ref[...], sb_ref[...]
    half = ROT_DIM // 2
    for w_ref, o_ref in ((wq_ref, q_ref), (wk_ref, k_ref)):
        for c in range(0, ATT_WIDTH, COL_CHUNK):
            acc = _dot(hn, w_ref[:, c:c + COL_CHUNK])
            for j in range(0, COL_CHUNK, LANES):
                t = acc[:, j:j + LANES]
                nxt = pltpu.roll(t, LANES - half, axis=1)
                prv = pltpu.roll(t, half, axis=1)
                o_ref[:, c + j:c + j + LANES] = (t * cos + nxt * sa + prv * sb).astype(BF16)
    for c in range(0, ATT_WIDTH, COL_CHUNK):
        v_ref[:, c:c + COL_CHUNK] = _dot(hn, wv_ref[:, c:c + COL_CHUNK]).astype(BF16)

    for c in range(0, 2 * D_MODEL, COL_CHUNK):
        g_ref[:, c:c + COL_CHUNK] = _dot(hn, wg_ref[:, c:c + COL_CHUNK]).astype(BF16)


def _rope_tables(s):
    half = ROT_DIM // 2
    pos = jnp.arange(s, dtype=F32)
    inv_freq = jnp.float32(ROPE_THETA) ** (-jnp.arange(0, ROT_DIM, 2, dtype=F32) / ROT_DIM)
    ang = pos[:, None] * inv_freq[None, :]
    ang = jnp.concatenate([ang, ang], axis=-1)
    cos, sin = jnp.cos(ang), jnp.sin(ang)
    pad = ATT_HEAD_DIM - ROT_DIM
    cos_t = jnp.concatenate([cos, jnp.ones((s, pad), F32)], axis=-1)
    sa_t = jnp.concatenate([-sin[:, :half], jnp.zeros((s, ATT_HEAD_DIM - half), F32)], axis=-1)
    sb_t = jnp.concatenate([jnp.zeros((s, half), F32), sin[:, half:], jnp.zeros((s, pad), F32)], axis=-1)
    reps = LANES // ATT_HEAD_DIM
    return tuple(jnp.tile(t, (1, reps)) for t in (cos_t, sa_t, sb_t))


def _in_projection(x, norm1_w, w_in, conv_w, conv_b, dt_bias):
    b, s, d = x.shape
    tm = ROW_TILE
    wz, wxbc, wdt, wq, wk, wv, wg = jnp.split(w_in.astype(BF16), IN_OFFSETS, axis=-1)
    cos_t, sa_t, sb_t = _rope_tables(s)
    row = lambda bi, ti: (bi, ti, 0)
    tab = lambda bi, ti: (ti, 0)

    def out(width, dtype=BF16):
        return jax.ShapeDtypeStruct((b, s, width), dtype)

    in_specs = [
        pl.BlockSpec((None, tm, d), row),
        _resident((1, d)),
        _resident(wz.shape), _resident(wxbc.shape), _resident(wdt.shape), _resident((SSD_N_HEADS, d)),
        _resident(wq.shape), _resident(wk.shape), _resident(wv.shape), _resident(wg.shape),
        _resident((SSD_CONV, SSD_CONV_DIM)), _resident((1, SSD_CONV_DIM)),
        _resident((1, SSD_N_HEADS)), _resident((SSD_N_HEADS, 1)),
        pl.BlockSpec((tm, LANES), tab), pl.BlockSpec((tm, LANES), tab), pl.BlockSpec((tm, LANES), tab),
    ]
    out_specs = [
        pl.BlockSpec((None, tm, SSD_D_INNER), row),
        pl.BlockSpec((None, tm, SSD_CONV_DIM), row),
        pl.BlockSpec((None, tm, SSD_N_HEADS), row),
        pl.BlockSpec((None, SSD_N_HEADS, tm), lambda bi, ti: (bi, 0, ti)),
        pl.BlockSpec((None, tm, ATT_WIDTH), row),
        pl.BlockSpec((None, tm, ATT_WIDTH), row),
        pl.BlockSpec((None, tm, ATT_WIDTH), row),
        pl.BlockSpec((None, tm, 2 * D_MODEL), row),
    ]
    out_shape = [out(SSD_D_INNER), out(SSD_CONV_DIM), out(SSD_N_HEADS, F32),
                 jax.ShapeDtypeStruct((b, SSD_N_HEADS, s), F32),
                 out(ATT_WIDTH), out(ATT_WIDTH), out(ATT_WIDTH), out(2 * D_MODEL)]
    return pl.pallas_call(
        _inproj_kernel,
        grid=(b, s // tm),
        in_specs=in_specs,
        out_specs=out_specs,
        out_shape=out_shape,
        scratch_shapes=[pltpu.VMEM((SUBLANES, SSD_CONV_DIM), F32),
                        pltpu.VMEM((2, SSD_CONV - 1, tm + 2 * SUBLANES, COL_CHUNK), F32),
                        pltpu.VMEM(wxbc.shape, BF16)],
        compiler_params=pltpu.CompilerParams(
            dimension_semantics=("arbitrary", "arbitrary"), vmem_limit_bytes=VMEM_LIMIT_BIG),
        name="in_projection",
    )(x, norm1_w.reshape(1, d), wz, wxbc, wdt, wdt.T, wq, wk, wv, wg,
      conv_w, conv_b.reshape(1, -1),
      dt_bias.reshape(1, -1), dt_bias.reshape(-1, 1), cos_t, sa_t, sb_t)


def _ssd_kernel(xbc_ref, sz_ref, dt_ref, dtT_ref, alog_ref, alogT_ref, dsk_ref, nw_ref, e_ch_ref,
                y_ref, state_ref):
    l = SSD_CHUNK
    gw = SSD_GROUP_WIDTH
    hpg = SSD_HEADS_PER_GROUP
    p = SSD_HEAD_DIM

    @pl.when(pl.program_id(1) == 0)
    def _():
        state_ref[...] = jnp.zeros(state_ref.shape, F32)

    log2e = float(np.log2(np.e))
    dt = dt_ref[...]
    a = dt * (-log2e * jnp.exp(alog_ref[...]))
    aT = dtT_ref[...] * (-log2e * jnp.exp(alogT_ref[...]))

    ri = lax.broadcasted_iota(jnp.int32, (l, l), 0)
    ci = lax.broadcasted_iota(jnp.int32, (l, l), 1)
    lower = ri >= ci
    tri = jnp.where(lower, 1.0, 0.0).astype(BF16)
    triT = jnp.where(ri <= ci, 1.0, 0.0).astype(BF16)
    cs = sum(_dot(tri, piece) for piece in _split3(a))
    csT = sum(_dot(piece, triT) for piece in _split3(aT))
    dt_pieces = jnp.concatenate(_split3(dt), axis=1)
    cs_pieces = jnp.concatenate(_split3(cs), axis=1)

    lane_head = lax.broadcasted_iota(jnp.int32, (l, gw), 1) // p

    def group_steps(g):
        cols = slice(g * gw, (g + 1) * gw)
        v = {}

        def load_and_project():
            v["xg"] = xbc_ref[:, cols].astype(F32)
            v["bg"] = xbc_ref[:, SSD_D_INNER + g * SSD_D_STATE:SSD_D_INNER + (g + 1) * SSD_D_STATE]
            v["cg"] = xbc_ref[:, SSD_D_INNER + SSD_GN + g * SSD_D_STATE:
                              SSD_D_INNER + SSD_GN + (g + 1) * SSD_D_STATE]
            v["cb"] = _dot_nt(v["cg"], v["bg"])
            v["dt_g"] = _dot(dt_pieces, e_ch_ref[:, cols])
            v["cs_g"] = _dot(cs_pieces, e_ch_ref[:, cols])
            v["st"] = state_ref[g]
            v["y_off"] = _dot(v["cg"], v["st"].astype(BF16))

        def scale_inputs():
            tot_g = v["tot_g"] = v["cs_g"][l - 1:l, :]
            v["xdt"] = v["xg"] * v["dt_g"]
            v["xdec"] = (v["xdt"] * jnp.exp2(tot_g - v["cs_g"])).astype(BF16)
            v["y"] = v["xg"] * dsk_ref[:, cols] + v["y_off"] * jnp.exp2(v["cs_g"])

        def head_pair(r0):
            mix, rhs = [], []
            for r in (r0, r0 + 1):
                h = g * hpg + r
                decay = jnp.exp2(cs[:, h:h + 1] - csT[h:h + 1, :])
                mix.append(jnp.where(lower, v["cb"] * decay, 0.0).astype(BF16))
                rhs.append(jnp.where(lane_head == r, v["xdt"], 0.0).astype(BF16))
            v["y"] = v["y"] + _dot(jnp.concatenate(mix, axis=1), jnp.concatenate(rhs, axis=0))

        def update_state():
            state_ref[g] = v["st"] * jnp.exp2(v["tot_g"]) + _dot_tn(v["bg"], v["xdec"])

        def gate_and_norm():
            y = v["y"] * sz_ref[:, cols].astype(F32)
            ms = jnp.mean(y * y, axis=-1, keepdims=True)
            y_ref[:, cols] = (y * lax.rsqrt(ms + RMS_EPS) * nw_ref[:, cols]).astype(BF16)

        return ([load_and_project, scale_inputs] +
                [functools.partial(head_pair, r0) for r0 in range(0, hpg, 2)] +
                [update_state, gate_and_norm])

    pending = [group_steps(g) for g in range(SSD_N_GROUPS)]
    n_steps = len(pending[0])
    for t in range(SSD_N_GROUPS + n_steps - 1):
        for g in range(SSD_N_GROUPS):
            if 0 <= t - g < n_steps:
                pending[g][t - g]()


def _ssd(xbc, sz, dt, dtT, a_log, d_skip, norm_w):
    b, s, _ = xbc.shape
    l = SSD_CHUNK
    row = lambda bi, ci: (bi, ci, 0)
    piece_head = np.arange(3 * SSD_N_HEADS)[:, None] % SSD_N_HEADS
    e_ch = jnp.asarray(piece_head == np.arange(SSD_D_INNER)[None, :] // SSD_HEAD_DIM, BF16)
    return pl.pallas_call(
        _ssd_kernel,
        grid=(b, s // l),
        in_specs=[
            pl.BlockSpec((None, l, SSD_CONV_DIM), row),
            pl.BlockSpec((None, l, SSD_D_INNER), row),
            pl.BlockSpec((None, l, SSD_N_HEADS), row),
            pl.BlockSpec((None, SSD_N_HEADS, l), lambda bi, ci: (bi, 0, ci)),
            _resident((1, SSD_N_HEADS)), _resident((SSD_N_HEADS, 1)),
            _resident((1, SSD_D_INNER)), _resident((1, SSD_D_INNER)),
            _resident(e_ch.shape),
        ],
        out_specs=pl.BlockSpec((None, l, SSD_D_INNER), row),
        out_shape=jax.ShapeDtypeStruct((b, s, SSD_D_INNER), BF16),
        scratch_shapes=[pltpu.VMEM((SSD_N_GROUPS, SSD_D_STATE, SSD_GROUP_WIDTH), F32)],
        compiler_params=pltpu.CompilerParams(
            dimension_semantics=("arbitrary", "arbitrary"), vmem_limit_bytes=VMEM_LIMIT_SMALL),
        name="ssd_scan",
    )(xbc, sz, dt, dtT, a_log.reshape(1, -1), a_log.reshape(-1, 1),
      jnp.repeat(d_skip, SSD_HEAD_DIM).reshape(1, -1), norm_w.reshape(1, -1), e_ch)


VT_ROWS = ATT_HEAD_DIM + 16


def _moba_kernel(q_ref, k_ref, v_ref, o_ref, vt_ref, s_ref, p_ref):
    s = k_ref.shape[0]
    blk = MOBA_BLOCK
    nb = s // blk
    nbp = -(-nb // SUBLANES) * SUBLANES
    hd = ATT_HEAD_DIM

    for n in range(nb):
        vt = v_ref[n * blk:(n + 1) * blk, :].astype(F32).T
        for h in range(2):
            vt_ref[h, 0:hd, n * blk:(n + 1) * blk] = vt[h * hd:(h + 1) * hd].astype(BF16)
    vt_ref[:, hd:VT_ROWS, :] = jnp.ones((2, VT_ROWS - hd, s), BF16)

    arow = lax.broadcasted_iota(jnp.int32, (nbp, s), 0)
    acol = lax.broadcasted_iota(jnp.int32, (nbp, s), 1) // blk
    avg = jnp.where(arow == acol, 1.0 / blk, 0.0).astype(BF16)
    km = _dot(avg, k_ref[...])
    km_hi = km.astype(BF16)
    km_lo = (km - km_hi.astype(F32)).astype(BF16)

    head_row = lax.broadcasted_iota(jnp.int32, (LANES, blk), 0) // hd
    n_idx = lax.broadcasted_iota(jnp.int32, (nbp, blk), 0)
    causal_t = (lax.broadcasted_iota(jnp.int32, (blk, blk), 0) <=
                lax.broadcasted_iota(jnp.int32, (blk, blk), 1))
    scale = hd ** -0.5 * float(np.log2(np.e))

    n_slots = s_ref.shape[0]
    units = [(qi, h) for qi in range(nb) for h in range(2)]
    state = {}

    def score_steps(u):
        qi, h = units[u]
        slot = u % n_slots
        st = state[u] = {"biases": [], "m": None}

        def prologue():
            if h == 0:
                st["qt"] = q_ref[qi * blk:(qi + 1) * blk, :].astype(F32).T * scale
            else:
                st["qt"] = state[u - 1]["qt"]
            qt_h = st["qt_h"] = jnp.where(head_row == h, st["qt"], 0.0).astype(BF16)
            if qi > 0:
                gt = _dot(km_hi, qt_h) + _dot(km_lo, qt_h)
                rank = jnp.zeros((nbp, blk), jnp.int32)
                for m in range(qi):
                    gm = gt[m:m + 1, :]
                    beats = (gm > gt) | ((gm == gt) & (n_idx > m))
                    rank = rank + jnp.where(beats, 1, 0)
                st["bias_t"] = jnp.where(rank >= MOBA_TOPK, NEG_INF, 0.0)

        def block(n):
            rows = slice(n * blk, (n + 1) * blk)
            sc = _dot(k_ref[rows, :], st["qt_h"])
            if n == qi:
                sc = jnp.where(causal_t, sc, NEG_INF)
                b_n = jnp.zeros((1, blk), F32)
            else:
                b_n = st["bias_t"][n:n + 1, :]
            s_ref[slot, rows, :] = sc
            m_n = jnp.max(sc, axis=0, keepdims=True) + b_n
            st["m"] = m_n if st["m"] is None else jnp.maximum(st["m"], m_n)
            st["biases"].append(b_n)

        return [prologue] + [functools.partial(block, n) for n in range(qi + 1)]

    def prob_steps(u):
        qi, _ = units[u]
        slot = u % n_slots
        st = state[u]

        def block(n):
            rows = slice(n * blk, (n + 1) * blk)
            p_ref[slot, rows, :] = jnp.exp2(s_ref[slot, rows, :] - (st["m"] - st["biases"][n])).astype(BF16)

        return [functools.partial(block, n) for n in range(qi + 1)]

    def value_steps(u):
        qi, h = units[u]
        slot = u % n_slots
        st = state[u]
        st["acc"] = None

        def block(n):
            rows = slice(n * blk, (n + 1) * blk)
            part = _dot(vt_ref[h, :, rows], p_ref[slot, rows, :])
            st["acc"] = part if st["acc"] is None else st["acc"] + part

        def epilogue():
            acc = st["acc"]
            st["out"] = acc[0:hd] * (1.0 / acc[hd:hd + 1])
            if h == 1:
                pair = jnp.concatenate([state[u - 1]["out"], st["out"]], axis=0)
                o_ref[qi * blk:(qi + 1) * blk, :] = pair.T.astype(BF16)
                del state[u - 1]

        return [functools.partial(block, n) for n in range(qi + 1)] + [epilogue]

    for t in range(len(units) + 2):
        streams = [score_steps(t) if t < len(units) else [],
                   prob_steps(t - 1) if 0 <= t - 1 < len(units) else [],
                   value_steps(t - 2) if 0 <= t - 2 < len(units) else []]
        for i in range(max(len(stream) for stream in streams)):
            for stream in streams:
                if i < len(stream):
                    stream[i]()


def _moba(q, k, v):
    b, s, _ = q.shape
    spec = pl.BlockSpec((None, s, LANES), lambda bi, hi: (bi, 0, hi))
    return pl.pallas_call(
        _moba_kernel,
        grid=(b, ATT_N_HEADS // 2),
        in_specs=[spec, spec, spec],
        out_specs=spec,
        out_shape=jax.ShapeDtypeStruct((b, s, ATT_WIDTH), BF16),
        scratch_shapes=[pltpu.VMEM((2, VT_ROWS, s), BF16), pltpu.VMEM((4, s, MOBA_BLOCK), F32),
                        pltpu.VMEM((4, s, MOBA_BLOCK), BF16)],
        compiler_params=pltpu.CompilerParams(
            dimension_semantics=("arbitrary", "arbitrary"), vmem_limit_bytes=VMEM_LIMIT_SMALL),
        name="moba_attention",
    )(q, k, v)


def _mixer_kernel(x_ref, ys_ref, ya_ref, g_ref, bg_ref, wsp_ref, wap_ref, wo_ref, n2_ref, wup_ref, fcw_ref,
                  fcb_ref, wdn_ref, nf_ref, o_ref, up_ref):
    tm = x_ref.shape[0]
    halo = SUBLANES

    @pl.when(pl.program_id(1) == 0)
    def _():
        up_ref[0:halo, :] = jnp.zeros((halo, FFN_HIDDEN), F32)

    g_ssd = _sigmoid(g_ref[:, :D_MODEL].astype(F32) + bg_ref[:, :D_MODEL])
    g_att = _sigmoid(g_ref[:, D_MODEL:].astype(F32) + bg_ref[:, D_MODEL:])
    merged = g_ssd * _dot(ys_ref[...], wsp_ref[...]) + g_att * _dot(ya_ref[...], wap_ref[...])
    x1 = x_ref[...] + _dot(merged.astype(BF16), wo_ref[...])

    var = jnp.mean(x1 * x1, axis=-1, keepdims=True)
    h2 = (x1 * lax.rsqrt(var + RMS_EPS) * n2_ref[...]).astype(BF16)

    n_chunk = FFN_HIDDEN // 2
    y = x1
    for c in range(0, FFN_HIDDEN, n_chunk):
        cs = slice(c, c + n_chunk)
        up_ref[halo:halo + tm, cs] = _dot(h2, wup_ref[:, cs])
        conv = fcb_ref[:, cs]
        for j in range(FFN_CONV):
            off = halo - (FFN_CONV - 1) + j
            conv = conv + fcw_ref[j:j + 1, cs] * up_ref[off:off + tm, cs]
        gate = _dot(h2, wup_ref[:, FFN_HIDDEN + c:FFN_HIDDEN + c + n_chunk])
        act = (conv * _sigmoid(conv) * gate).astype(BF16)
        up_ref[0:halo, cs] = up_ref[tm:tm + halo, cs]
        y = y + _dot(act, wdn_ref[cs, :])

    var = jnp.mean(y * y, axis=-1, keepdims=True)
    o_ref[...] = y * lax.rsqrt(var + RMS_EPS) * nf_ref[...]


def _mixer(x, y_ssd, y_att, gates, b_gate, w_ssd_proj, w_att_proj, w_out, norm2_w, w_ffn_up, ffn_conv_w, ffn_conv_b,
           w_ffn_down, norm_f_w):
    b, s, d = x.shape
    tm = ROW_TILE
    row = lambda bi, ti: (bi, ti, 0)
    wsp, wap, wo, wup, wdn = (w.astype(BF16) for w in (w_ssd_proj, w_att_proj, w_out, w_ffn_up, w_ffn_down))
    return pl.pallas_call(
        _mixer_kernel,
        grid=(b, s // tm),
        in_specs=[
            pl.BlockSpec((None, tm, d), row),
            pl.BlockSpec((None, tm, SSD_D_INNER), row),
            pl.BlockSpec((None, tm, ATT_WIDTH), row),
            pl.BlockSpec((None, tm, 2 * D_MODEL), row),
            _resident((1, 2 * D_MODEL)),
            _resident(wsp.shape), _resident(wap.shape), _resident(wo.shape), _resident((1, d)),
            _resident(wup.shape), _resident((FFN_CONV, FFN_HIDDEN)), _resident((1, FFN_HIDDEN)),
            _resident(wdn.shape), _resident((1, d)),
        ],
        out_specs=pl.BlockSpec((None, tm, d), row),
        out_shape=jax.ShapeDtypeStruct((b, s, d), F32),
        scratch_shapes=[pltpu.VMEM((tm + SUBLANES, FFN_HIDDEN), F32)],
        compiler_params=pltpu.CompilerParams(
            dimension_semantics=("arbitrary", "arbitrary"), vmem_limit_bytes=VMEM_LIMIT_BIG),
        name="merge_convffn",
    )(x, y_ssd, y_att, gates, b_gate.reshape(1, -1), wsp, wap, wo, norm2_w.reshape(1, d), wup, ffn_conv_w,
      ffn_conv_b.reshape(1, -1), wdn, norm_f_w.reshape(1, d))


def kernel(x, norm1_w, w_in, b_gate, ssd_conv_w, ssd_conv_b, ssd_dt_bias, ssd_a_log, ssd_d, ssd_norm_w,
           w_ssd_proj, w_att_proj, w_out, norm2_w, w_ffn_up, ffn_conv_w, ffn_conv_b, w_ffn_down, norm_f_w):
    assert norm1_w.shape[0] == 1, "single-layer block"
    sz, xbc, dt, dtT, q, k, v, gates = _in_projection(
        x, norm1_w[0], w_in[0], ssd_conv_w[0], ssd_conv_b[0], ssd_dt_bias[0])
    y_ssd = _ssd(xbc, sz, dt, dtT, ssd_a_log[0], ssd_d[0], ssd_norm_w[0])
    y_att = _moba(q, k, v)
    return _mixer(x, y_ssd, y_att, gates, b_gate[0], w_ssd_proj[0], w_att_proj[0], w_out[0], norm2_w[0], w_ffn_up[0],
                  ffn_conv_w[0], ffn_conv_b[0], w_ffn_down[0], norm_f_w)
```

```python
import functools

import jax
import jax.numpy as jnp
import numpy as np
from jax import lax
from jax.experimental import pallas as pl
from jax.experimental.pallas import tpu as pltpu

D_MODEL = 1024
RMS_EPS = 1e-6
NEG_INF = -1e30

SSD_D_INNER = 2048
SSD_HEAD_DIM = 64
SSD_N_HEADS = 32
SSD_N_GROUPS = 8
SSD_HEADS_PER_GROUP = SSD_N_HEADS // SSD_N_GROUPS
SSD_GROUP_WIDTH = SSD_D_INNER // SSD_N_GROUPS
SSD_D_STATE = 128
SSD_CONV = 4
SSD_CHUNK = 128
SSD_GN = SSD_N_GROUPS * SSD_D_STATE
SSD_CONV_DIM = SSD_D_INNER + 2 * SSD_GN

ATT_HEAD_DIM = 64
ATT_N_HEADS = 16
ATT_WIDTH = ATT_N_HEADS * ATT_HEAD_DIM
MOBA_BLOCK = 256
MOBA_TOPK = 3
ROPE_THETA = 500000.0
ROT_DIM = ATT_HEAD_DIM // 4

FFN_HIDDEN = 2816
FFN_CONV = 3

IN_SIZES = (SSD_D_INNER, SSD_CONV_DIM, SSD_N_HEADS, ATT_WIDTH, ATT_WIDTH, ATT_WIDTH, 2 * D_MODEL)
IN_OFFSETS = tuple(int(v) for v in np.cumsum(IN_SIZES)[:-1])

LANES = 128
SUBLANES = 8
VMEM_LIMIT_BIG = 56 * 1024 * 1024
VMEM_LIMIT_SMALL = 40 * 1024 * 1024

ROW_TILE = 256
COL_CHUNK = 512

BF16 = jnp.bfloat16
F32 = jnp.float32


def _dot(a, b):
    return jnp.dot(a, b, preferred_element_type=F32)


def _dot_nt(a, b):
    return lax.dot_general(a, b, (((1,), (1,)), ((), ())), preferred_element_type=F32)


def _dot_tn(a, b):
    return lax.dot_general(a, b, (((0,), (0,)), ((), ())), preferred_element_type=F32)


def _sigmoid(x):
    return 1.0 / (1.0 + jnp.exp2(x * float(-np.log2(np.e))))


def _softplus(x):
    return jnp.maximum(x, 0.0) + jnp.log1p(jnp.exp(-jnp.abs(x)))


def _split3(x):
    hi = x.astype(BF16)
    r = x - hi.astype(F32)
    mid = r.astype(BF16)
    lo = (r - mid.astype(F32)).astype(BF16)
    return hi, mid, lo


def _resident(shape):
    nd = len(shape)
    return pl.BlockSpec(shape, lambda *_: (0,) * nd, pipeline_mode=pl.Buffered(1))


def _inproj_kernel(x_ref, n1_ref, wz_ref, wxbc_ref, wdt_ref, wdtT_ref, wq_ref, wk_ref, wv_ref, wg_ref,
                   cw_ref, cb_ref, dtb_ref, dtbT_ref, cos_ref, sa_ref, sb_ref,
                   sz_ref, xbc_ref, dt_ref, dtT_ref, q_ref, k_ref, v_ref, g_ref, raw_ref, wcopy_ref, stage_ref):
    tm = x_ref.shape[0]
    halo = SUBLANES

    @pl.when(pl.program_id(1) == 0)
    def _():
        raw_ref[...] = jnp.zeros(raw_ref.shape, F32)

    @pl.when((pl.program_id(0) == 0) & (pl.program_id(1) == 0))
    def _():
        wcopy_ref[...] = wxbc_ref[...]

    xf = x_ref[...]
    var = jnp.mean(xf * xf, axis=-1, keepdims=True)
    hn = (xf * lax.rsqrt(var + RMS_EPS) * n1_ref[...]).astype(BF16)

    for c in range(0, SSD_D_INNER, COL_CHUNK):
        acc = _dot(hn, wz_ref[:, c:c + COL_CHUNK])
        sz_ref[:, c:c + COL_CHUNK] = (acc * _sigmoid(acc)).astype(BF16)

    for c in range(0, SSD_CONV_DIM, COL_CHUNK):
        cs = slice(c, c + COL_CHUNK)
        acc = _dot(hn, wcopy_ref[:, cs])
        for j in range(COL_CHUNK // LANES):
            lanes = slice(c + j * LANES, c + (j + 1) * LANES)
            t = acc[:, j * LANES:(j + 1) * LANES]
            col = (c // LANES + j) % stage_ref.shape[0]
            stage_ref[col, 0:halo, :] = raw_ref[:, lanes]
            stage_ref[col, halo:halo + tm, :] = t
            conv = cb_ref[:, lanes] + cw_ref[SSD_CONV - 1:SSD_CONV, lanes] * t
            for k in range(1, SSD_CONV):
                conv = conv + (cw_ref[SSD_CONV - 1 - k:SSD_CONV - k, lanes] *
                               stage_ref[col, halo - k:halo - k + tm, :])
            xbc_ref[:, lanes] = (conv * _sigmoid(conv)).astype(BF16)
            raw_ref[:, lanes] = t[tm - halo:tm, :]

    dt_ref[...] = _softplus(_dot(hn, wdt_ref[...]) + dtb_ref[...])
    dtT_ref[...] = _softplus(_dot_nt(wdtT_ref[...], hn) + dtbT_ref[...])

    cos, sa, sb = cos_ref[...], sa_ref[...], sb_ref[...]
    half = ROT_DIM // 2
    for w_ref, o_ref in ((wq_ref, q_ref), (wk_ref, k_ref)):
        for c in range(0, ATT_WIDTH, COL_CHUNK):
            acc = _dot(hn, w_ref[:, c:c + COL_CHUNK])
            for j in range(0, COL_CHUNK, LANES):
                t = acc[:, j:j + LANES]
                nxt = pltpu.roll(t, LANES - half, axis=1)
                prv = pltpu.roll(t, half, axis=1)
                o_ref[:, c + j:c + j + LANES] = (t * cos + nxt * sa + prv * sb).astype(BF16)
    for c in range(0, ATT_WIDTH, COL_CHUNK):
        v_ref[:, c:c + COL_CHUNK] = _dot(hn, wv_ref[:, c:c + COL_CHUNK]).astype(BF16)

    for c in range(0, 2 * D_MODEL, COL_CHUNK):
        g_ref[:, c:c + COL_CHUNK] = _dot(hn, wg_ref[:, c:c + COL_CHUNK]).astype(BF16)


def _rope_tables(s):
    half = ROT_DIM // 2
    pos = jnp.arange(s, dtype=F32)
    inv_freq = jnp.float32(ROPE_THETA) ** (-jnp.arange(0, ROT_DIM, 2, dtype=F32) / ROT_DIM)
    ang = pos[:, None] * inv_freq[None, :]
    ang = jnp.concatenate([ang, ang], axis=-1)
    cos, sin = jnp.cos(ang), jnp.sin(ang)
    pad = ATT_HEAD_DIM - ROT_DIM
    cos_t = jnp.concatenate([cos, jnp.ones((s, pad), F32)], axis=-1)
    sa_t = jnp.concatenate([-sin[:, :half], jnp.zeros((s, ATT_HEAD_DIM - half), F32)], axis=-1)
    sb_t = jnp.concatenate([jnp.zeros((s, half), F32), sin[:, half:], jnp.zeros((s, pad), F32)], axis=-1)
    reps = LANES // ATT_HEAD_DIM
    return tuple(jnp.tile(t, (1, reps)) for t in (cos_t, sa_t, sb_t))


def _in_projection(x, norm1_w, w_in, conv_w, conv_b, dt_bias):
    b, s, d = x.shape
    tm = ROW_TILE
    wz, wxbc, wdt, wq, wk, wv, wg = jnp.split(w_in.astype(BF16), IN_OFFSETS, axis=-1)
    cos_t, sa_t, sb_t = _rope_tables(s)
    row = lambda bi, ti: (bi, ti, 0)
    tab = lambda bi, ti: (ti, 0)

    def out(width, dtype=BF16):
        return jax.ShapeDtypeStruct((b, s, width), dtype)

    in_specs = [
        pl.BlockSpec((None, tm, d), row),
        _resident((1, d)),
        _resident(wz.shape), _resident(wxbc.shape), _resident(wdt.shape), _resident((SSD_N_HEADS, d)),
        _resident(wq.shape), _resident(wk.shape), _resident(wv.shape), _resident(wg.shape),
        _resident((SSD_CONV, SSD_CONV_DIM)), _resident((1, SSD_CONV_DIM)),
        _resident((1, SSD_N_HEADS)), _resident((SSD_N_HEADS, 1)),
        pl.BlockSpec((tm, LANES), tab), pl.BlockSpec((tm, LANES), tab), pl.BlockSpec((tm, LANES), tab),
    ]
    out_specs = [
        pl.BlockSpec((None, tm, SSD_D_INNER), row),
        pl.BlockSpec((None, tm, SSD_CONV_DIM), row),
        pl.BlockSpec((None, tm, SSD_N_HEADS), row),
        pl.BlockSpec((None, SSD_N_HEADS, tm), lambda bi, ti: (bi, 0, ti)),
        pl.BlockSpec((None, tm, ATT_WIDTH), row),
        pl.BlockSpec((None, tm, ATT_WIDTH), row),
        pl.BlockSpec((None, tm, ATT_WIDTH), row),
        pl.BlockSpec((None, tm, 2 * D_MODEL), row),
    ]
    out_shape = [out(SSD_D_INNER), out(SSD_CONV_DIM), out(SSD_N_HEADS, F32),
                 jax.ShapeDtypeStruct((b, SSD_N_HEADS, s), F32),
                 out(ATT_WIDTH), out(ATT_WIDTH), out(ATT_WIDTH), out(2 * D_MODEL)]
    return pl.pallas_call(
        _inproj_kernel,
        grid=(b, s // tm),
        in_specs=in_specs,
        out_specs=out_specs,
        out_shape=out_shape,
        scratch_shapes=[pltpu.VMEM((SUBLANES, SSD_CONV_DIM), F32), pltpu.VMEM(wxbc.shape, BF16),
                        pltpu.VMEM((8, tm + SUBLANES, LANES), F32)],
        compiler_params=pltpu.CompilerParams(
            dimension_semantics=("arbitrary", "arbitrary"), vmem_limit_bytes=VMEM_LIMIT_BIG),
        name="in_projection",
    )(x, norm1_w.reshape(1, d), wz, wxbc, wdt, wdt.T, wq, wk, wv, wg,
      conv_w, conv_b.reshape(1, -1),
      dt_bias.reshape(1, -1), dt_bias.reshape(-1, 1), cos_t, sa_t, sb_t)


def _ssd_kernel(xbc_ref, sz_ref, dt_ref, dtT_ref, alog_ref, alogT_ref, dsk_ref, nw_ref, e_ch_ref,
                y_ref, state_ref):
    l = SSD_CHUNK
    gw = SSD_GROUP_WIDTH
    hpg = SSD_HEADS_PER_GROUP
    p = SSD_HEAD_DIM

    @pl.when(pl.program_id(1) == 0)
    def _():
        state_ref[...] = jnp.zeros(state_ref.shape, F32)

    log2e = float(np.log2(np.e))
    dt = dt_ref[...]
    a = dt * (-log2e * jnp.exp(alog_ref[...]))
    aT = dtT_ref[...] * (-log2e * jnp.exp(alogT_ref[...]))

    ri = lax.broadcasted_iota(jnp.int32, (l, l), 0)
    ci = lax.broadcasted_iota(jnp.int32, (l, l), 1)
    lower = ri >= ci
    tri = jnp.where(lower, 1.0, 0.0).astype(BF16)
    triT = jnp.where(ri <= ci, 1.0, 0.0).astype(BF16)
    cs = sum(_dot(tri, piece) for piece in _split3(a))
    csT = sum(_dot(piece, triT) for piece in _split3(aT))
    dt_pieces = jnp.concatenate(_split3(dt), axis=1)
    cs_pieces = jnp.concatenate(_split3(cs), axis=1)

    lane_head = lax.broadcasted_iota(jnp.int32, (l, gw), 1) // p

    def group_steps(g):
        cols = slice(g * gw, (g + 1) * gw)
        v = {}

        def load_and_project():
            v["xg"] = xbc_ref[:, cols].astype(F32)
            v["bg"] = xbc_ref[:, SSD_D_INNER + g * SSD_D_STATE:SSD_D_INNER + (g + 1) * SSD_D_STATE]
            v["cg"] = xbc_ref[:, SSD_D_INNER + SSD_GN + g * SSD_D_STATE:
                              SSD_D_INNER + SSD_GN + (g + 1) * SSD_D_STATE]
            v["cb"] = _dot_nt(v["cg"], v["bg"])
            v["dt_g"] = _dot(dt_pieces, e_ch_ref[:, cols])
            v["cs_g"] = _dot(cs_pieces, e_ch_ref[:, cols])
            v["st"] = state_ref[g]
            v["y_off"] = _dot(v["cg"], v["st"].astype(BF16))

        def scale_inputs():
            tot_g = v["tot_g"] = v["cs_g"][l - 1:l, :]
            v["xdt"] = v["xg"] * v["dt_g"]
            v["xdec"] = (v["xdt"] * jnp.exp2(tot_g - v["cs_g"])).astype(BF16)
            v["y"] = v["xg"] * dsk_ref[:, cols] + v["y_off"] * jnp.exp2(v["cs_g"])

        def head_pair(r0):
            mix, rhs = [], []
            for r in (r0, r0 + 1):
                h = g * hpg + r
                decay = jnp.exp2(cs[:, h:h + 1] - csT[h:h + 1, :])
                mix.append(jnp.where(lower, v["cb"] * decay, 0.0).astype(BF16))
                rhs.append(jnp.where(lane_head == r, v["xdt"], 0.0).astype(BF16))
            v["y"] = v["y"] + _dot(jnp.concatenate(mix, axis=1), jnp.concatenate(rhs, axis=0))

        def update_state():
            state_ref[g] = v["st"] * jnp.exp2(v["tot_g"]) + _dot_tn(v["bg"], v["xdec"])

        def gate_and_norm():
            y = v["y"] * sz_ref[:, cols].astype(F32)
            ms = jnp.mean(y * y, axis=-1, keepdims=True)
            y_ref[:, cols] = (y * lax.rsqrt(ms + RMS_EPS) * nw_ref[:, cols]).astype(BF16)

        return ([load_and_project, scale_inputs] +
                [functools.partial(head_pair, r0) for r0 in range(0, hpg, 2)] +
                [update_state, gate_and_norm])

    pending = [group_steps(g) for g in range(SSD_N_GROUPS)]
    n_steps = len(pending[0])
    for t in range(SSD_N_GROUPS + n_steps - 1):
        for g in range(SSD_N_GROUPS):
            if 0 <= t - g < n_steps:
                pending[g][t - g]()


def _ssd(xbc, sz, dt, dtT, a_log, d_skip, norm_w):
    b, s, _ = xbc.shape
    l = SSD_CHUNK
    row = lambda bi, ci: (bi, ci, 0)
    piece_head = np.arange(3 * SSD_N_HEADS)[:, None] % SSD_N_HEADS
    e_ch = jnp.asarray(piece_head == np.arange(SSD_D_INNER)[None, :] // SSD_HEAD_DIM, BF16)
    return pl.pallas_call(
        _ssd_kernel,
        grid=(b, s // l),
        in_specs=[
            pl.BlockSpec((None, l, SSD_CONV_DIM), row),
            pl.BlockSpec((None, l, SSD_D_INNER), row),
            pl.BlockSpec((None, l, SSD_N_HEADS), row),
            pl.BlockSpec((None, SSD_N_HEADS, l), lambda bi, ci: (bi, 0, ci)),
            _resident((1, SSD_N_HEADS)), _resident((SSD_N_HEADS, 1)),
            _resident((1, SSD_D_INNER)), _resident((1, SSD_D_INNER)),
            _resident(e_ch.shape),
        ],
        out_specs=pl.BlockSpec((None, l, SSD_D_INNER), row),
        out_shape=jax.ShapeDtypeStruct((b, s, SSD_D_INNER), BF16),
        scratch_shapes=[pltpu.VMEM((SSD_N_GROUPS, SSD_D_STATE, SSD_GROUP_WIDTH), F32)],
        compiler_params=pltpu.CompilerParams(
            dimension_semantics=("arbitrary", "arbitrary"), vmem_limit_bytes=VMEM_LIMIT_SMALL),
        name="ssd_scan",
    )(xbc, sz, dt, dtT, a_log.reshape(1, -1), a_log.reshape(-1, 1),
      jnp.repeat(d_skip, SSD_HEAD_DIM).reshape(1, -1), norm_w.reshape(1, -1), e_ch)


VT_ROWS = ATT_HEAD_DIM + 16


def _moba_kernel(q_ref, k_ref, v_ref, o_ref, vt_ref, s_ref, p_ref):
    s = k_ref.shape[0]
    blk = MOBA_BLOCK
    nb = s // blk
    nbp = -(-nb // SUBLANES) * SUBLANES
    hd = ATT_HEAD_DIM

    for n in range(nb):
        vt = v_ref[n * blk:(n + 1) * blk, :].astype(F32).T
        for h in range(2):
            vt_ref[h, 0:hd, n * blk:(n + 1) * blk] = vt[h * hd:(h + 1) * hd].astype(BF16)
    vt_ref[:, hd:VT_ROWS, :] = jnp.ones((2, VT_ROWS - hd, s), BF16)

    arow = lax.broadcasted_iota(jnp.int32, (nbp, s), 0)
    acol = lax.broadcasted_iota(jnp.int32, (nbp, s), 1) // blk
    avg = jnp.where(arow == acol, 1.0 / blk, 0.0).astype(BF16)
    km = _dot(avg, k_ref[...])
    km_hi = km.astype(BF16)
    km_lo = (km - km_hi.astype(F32)).astype(BF16)

    head_row = lax.broadcasted_iota(jnp.int32, (LANES, blk), 0) // hd
    n_idx = lax.broadcasted_iota(jnp.int32, (nbp, blk), 0)
    causal_t = (lax.broadcasted_iota(jnp.int32, (blk, blk), 0) <=
                lax.broadcasted_iota(jnp.int32, (blk, blk), 1))
    scale = hd ** -0.5 * float(np.log2(np.e))

    n_slots = s_ref.shape[0]
    units = [(qi, h) for qi in range(nb) for h in range(2)]
    state = {}

    def score_steps(u):
        qi, h = units[u]
        slot = u % n_slots
        st = state[u] = {"biases": [], "m": None}

        def prologue():
            if h == 0:
                st["qt"] = q_ref[qi * blk:(qi + 1) * blk, :].astype(F32).T * scale
            else:
                st["qt"] = state[u - 1]["qt"]
            qt_h = st["qt_h"] = jnp.where(head_row == h, st["qt"], 0.0).astype(BF16)
            if qi > 0:
                gt = _dot(km_hi, qt_h) + _dot(km_lo, qt_h)
                rank = jnp.zeros((nbp, blk), jnp.int32)
                for m in range(qi):
                    gm = gt[m:m + 1, :]
                    beats = (gm > gt) | ((gm == gt) & (n_idx > m))
                    rank = rank + jnp.where(beats, 1, 0)
                st["bias_t"] = jnp.where(rank >= MOBA_TOPK, NEG_INF, 0.0)

        def block(n):
            rows = slice(n * blk, (n + 1) * blk)
            sc = _dot(k_ref[rows, :], st["qt_h"])
            if n == qi:
                sc = jnp.where(causal_t, sc, NEG_INF)
                b_n = jnp.zeros((1, blk), F32)
            else:
                b_n = st["bias_t"][n:n + 1, :]
            s_ref[slot, rows, :] = sc
            m_n = jnp.max(sc, axis=0, keepdims=True) + b_n
            st["m"] = m_n if st["m"] is None else jnp.maximum(st["m"], m_n)
            st["biases"].append(b_n)

        return [prologue] + [functools.partial(block, n) for n in range(qi + 1)]

    def prob_steps(u):
        qi, _ = units[u]
        slot = u % n_slots
        st = state[u]

        def block(n):
            rows = slice(n * blk, (n + 1) * blk)
            p_ref[slot, rows, :] = jnp.exp2(s_ref[slot, rows, :] - (st["m"] - st["biases"][n])).astype(BF16)

        return [functools.partial(block, n) for n in range(qi + 1)]

    def value_steps(u):
        qi, h = units[u]
        slot = u % n_slots
        st = state[u]
        st["acc"] = None

        def block(n):
            rows = slice(n * blk, (n + 1) * blk)
            part = _dot(vt_ref[h, :, rows], p_ref[slot, rows, :])
            st["acc"] = part if st["acc"] is None else st["acc"] + part

        def epilogue():
            acc = st["acc"]
            st["out"] = acc[0:hd] * (1.0 / acc[hd:hd + 1])
            if h == 1:
                pair = jnp.concatenate([state[u - 1]["out"], st["out"]], axis=0)
                o_ref[qi * blk:(qi + 1) * blk, :] = pair.T.astype(BF16)
                del state[u - 1]

        return [functools.partial(block, n) for n in range(qi + 1)] + [epilogue]

    for t in range(len(units) + 2):
        streams = [score_steps(t) if t < len(units) else [],
                   prob_steps(t - 1) if 0 <= t - 1 < len(units) else [],
                   value_steps(t - 2) if 0 <= t - 2 < len(units) else []]
        for i in range(max(len(stream) for stream in streams)):
            for stream in streams:
                if i < len(stream):
                    stream[i]()


def _moba(q, k, v):
    b, s, _ = q.shape
    spec = pl.BlockSpec((None, s, LANES), lambda bi, hi: (bi, 0, hi))
    return pl.pallas_call(
        _moba_kernel,
        grid=(b, ATT_N_HEADS // 2),
        in_specs=[spec, spec, spec],
        out_specs=spec,
        out_shape=jax.ShapeDtypeStruct((b, s, ATT_WIDTH), BF16),
        scratch_shapes=[pltpu.VMEM((2, VT_ROWS, s), BF16), pltpu.VMEM((4, s, MOBA_BLOCK), F32),
                        pltpu.VMEM((4, s, MOBA_BLOCK), BF16)],
        compiler_params=pltpu.CompilerParams(
            dimension_semantics=("arbitrary", "arbitrary"), vmem_limit_bytes=VMEM_LIMIT_SMALL),
        name="moba_attention",
    )(q, k, v)


def _mixer_kernel(x_ref, ys_ref, ya_ref, g_ref, bg_ref, wsp_ref, wap_ref, wo_ref, n2_ref, wup_ref, fcw_ref,
                  fcb_ref, wdn_ref, nf_ref, o_ref, up_ref):
    tm = x_ref.shape[0]
    halo = SUBLANES

    @pl.when(pl.program_id(1) == 0)
    def _():
        up_ref[:, 0:halo, :] = jnp.zeros((up_ref.shape[0], halo, LANES), F32)

    g_ssd = _sigmoid(g_ref[:, :D_MODEL].astype(F32) + bg_ref[:, :D_MODEL])
    g_att = _sigmoid(g_ref[:, D_MODEL:].astype(F32) + bg_ref[:, D_MODEL:])
    merged = g_ssd * _dot(ys_ref[...], wsp_ref[...]) + g_att * _dot(ya_ref[...], wap_ref[...])
    x1 = x_ref[...] + _dot(merged.astype(BF16), wo_ref[...])

    var = jnp.mean(x1 * x1, axis=-1, keepdims=True)
    h2 = (x1 * lax.rsqrt(var + RMS_EPS) * n2_ref[...]).astype(BF16)

    n_chunk = FFN_HIDDEN // 2
    y = x1
    for c in range(0, FFN_HIDDEN, n_chunk):
        cs = slice(c, c + n_chunk)
        up = _dot(h2, wup_ref[:, cs])
        convs = []
        for j in range(n_chunk // LANES):
            lanes = slice(c + j * LANES, c + (j + 1) * LANES)
            tile = (c // LANES) + j
            up_ref[tile, halo:halo + tm, :] = up[:, j * LANES:(j + 1) * LANES]
            conv = fcb_ref[:, lanes]
            for k in range(FFN_CONV):
                off = halo - (FFN_CONV - 1) + k
                conv = conv + fcw_ref[k:k + 1, lanes] * up_ref[tile, off:off + tm, :]
            up_ref[tile, 0:halo, :] = up_ref[tile, tm:tm + halo, :]
            convs.append(conv)
        conv = jnp.concatenate(convs, axis=1)
        gate = _dot(h2, wup_ref[:, FFN_HIDDEN + c:FFN_HIDDEN + c + n_chunk])
        act = (conv * _sigmoid(conv) * gate).astype(BF16)
        y = y + _dot(act, wdn_ref[cs, :])

    var = jnp.mean(y * y, axis=-1, keepdims=True)
    o_ref[...] = y * lax.rsqrt(var + RMS_EPS) * nf_ref[...]


def _mixer(x, y_ssd, y_att, gates, b_gate, w_ssd_proj, w_att_proj, w_out, norm2_w, w_ffn_up, ffn_conv_w, ffn_conv_b,
           w_ffn_down, norm_f_w):
    b, s, d = x.shape
    tm = ROW_TILE
    row = lambda bi, ti: (bi, ti, 0)
    wsp, wap, wo, wup, wdn = (w.astype(BF16) for w in (w_ssd_proj, w_att_proj, w_out, w_ffn_up, w_ffn_down))
    return pl.pallas_call(
        _mixer_kernel,
        grid=(b, s // tm),
        in_specs=[
            pl.BlockSpec((None, tm, d), row),
            pl.BlockSpec((None, tm, SSD_D_INNER), row),
            pl.BlockSpec((None, tm, ATT_WIDTH), row),
            pl.BlockSpec((None, tm, 2 * D_MODEL), row),
            _resident((1, 2 * D_MODEL)),
            _resident(wsp.shape), _resident(wap.shape), _resident(wo.shape), _resident((1, d)),
            _resident(wup.shape), _resident((FFN_CONV, FFN_HIDDEN)), _resident((1, FFN_HIDDEN)),
            _resident(wdn.shape), _resident((1, d)),
        ],
        out_specs=pl.BlockSpec((None, tm, d), row),
        out_shape=jax.ShapeDtypeStruct((b, s, d), F32),
        scratch_shapes=[pltpu.VMEM((FFN_HIDDEN // LANES, tm + SUBLANES, LANES), F32)],
        compiler_params=pltpu.CompilerParams(
            dimension_semantics=("arbitrary", "arbitrary"), vmem_limit_bytes=VMEM_LIMIT_BIG),
        name="merge_convffn",
    )(x, y_ssd, y_att, gates, b_gate.reshape(1, -1), wsp, wap, wo, norm2_w.reshape(1, d), wup, ffn_conv_w,
      ffn_conv_b.reshape(1, -1), wdn, norm_f_w.reshape(1, d))


def kernel(x, norm1_w, w_in, b_gate, ssd_conv_w, ssd_conv_b, ssd_dt_bias, ssd_a_log, ssd_d, ssd_norm_w,
           w_ssd_proj, w_att_proj, w_out, norm2_w, w_ffn_up, ffn_conv_w, ffn_conv_b, w_ffn_down, norm_f_w):
    assert norm1_w.shape[0] == 1, "single-layer block"
    sz, xbc, dt, dtT, q, k, v, gates = _in_projection(
        x, norm1_w[0], w_in[0], ssd_conv_w[0], ssd_conv_b[0], ssd_dt_bias[0])
    y_ssd = _ssd(xbc, sz, dt, dtT, ssd_a_log[0], ssd_d[0], ssd_norm_w[0])
    y_att = _moba(q, k, v)
    return _mixer(x, y_ssd, y_att, gates, b_gate[0], w_ssd_proj[0], w_att_proj[0], w_out[0], norm2_w[0], w_ffn_up[0],
                  ffn_conv_w[0], ffn_conv_b[0], w_ffn_down[0], norm_f_w)
```

```python
import functools

import jax
import jax.numpy as jnp
import numpy as np
from jax import lax
from jax.experimental import pallas as pl
from jax.experimental.pallas import tpu as pltpu

D_MODEL = 1024
RMS_EPS = 1e-6
NEG_INF = -1e30

SSD_D_INNER = 2048
SSD_HEAD_DIM = 64
SSD_N_HEADS = 32
SSD_N_GROUPS = 8
SSD_HEADS_PER_GROUP = SSD_N_HEADS // SSD_N_GROUPS
SSD_GROUP_WIDTH = SSD_D_INNER // SSD_N_GROUPS
SSD_D_STATE = 128
SSD_CONV = 4
SSD_CHUNK = 128
SSD_GN = SSD_N_GROUPS * SSD_D_STATE
SSD_CONV_DIM = SSD_D_INNER + 2 * SSD_GN

ATT_HEAD_DIM = 64
ATT_N_HEADS = 16
ATT_WIDTH = ATT_N_HEADS * ATT_HEAD_DIM
MOBA_BLOCK = 256
MOBA_TOPK = 3
ROPE_THETA = 500000.0
ROT_DIM = ATT_HEAD_DIM // 4

FFN_HIDDEN = 2816
FFN_CONV = 3

IN_SIZES = (SSD_D_INNER, SSD_CONV_DIM, SSD_N_HEADS, ATT_WIDTH, ATT_WIDTH, ATT_WIDTH, 2 * D_MODEL)
IN_OFFSETS = tuple(int(v) for v in np.cumsum(IN_SIZES)[:-1])

LANES = 128
SUBLANES = 8
VMEM_LIMIT_BIG = 56 * 1024 * 1024
VMEM_LIMIT_SMALL = 40 * 1024 * 1024

ROW_TILE = 256
COL_CHUNK = 512

BF16 = jnp.bfloat16
F32 = jnp.float32


def _dot(a, b):
    return jnp.dot(a, b, preferred_element_type=F32)


def _dot_nt(a, b):
    return lax.dot_general(a, b, (((1,), (1,)), ((), ())), preferred_element_type=F32)


def _dot_tn(a, b):
    return lax.dot_general(a, b, (((0,), (0,)), ((), ())), preferred_element_type=F32)


def _sigmoid(x):
    return 1.0 / (1.0 + jnp.exp2(x * float(-np.log2(np.e))))


def _softplus(x):
    return jnp.maximum(x, 0.0) + jnp.log1p(jnp.exp(-jnp.abs(x)))


def _split3(x):
    hi = x.astype(BF16)
    r = x - hi.astype(F32)
    mid = r.astype(BF16)
    lo = (r - mid.astype(F32)).astype(BF16)
    return hi, mid, lo


def _resident(shape):
    nd = len(shape)
    return pl.BlockSpec(shape, lambda *_: (0,) * nd, pipeline_mode=pl.Buffered(1))


def _inproj_kernel(x_ref, xnext_ref, n1_ref, wz_ref, wxbc_ref, wdt_ref, wdtT_ref, wq_ref, wk_ref, wv_ref, wg_ref,
                   cw_ref, cb_ref, dtb_ref, dtbT_ref, cos_ref, sa_ref, sb_ref,
                   sz_ref, xbc_ref, dt_ref, dtT_ref, q_ref, k_ref, v_ref, g_ref,
                   raw_ref, wcopy_ref, stage_ref, hn_ref):
    tm = x_ref.shape[0]
    halo = SUBLANES

    def normed(tile_ref):
        xf = tile_ref[...]
        var = jnp.mean(xf * xf, axis=-1, keepdims=True)
        return (xf * lax.rsqrt(var + RMS_EPS) * n1_ref[...]).astype(BF16)

    @pl.when(pl.program_id(1) == 0)
    def _():
        raw_ref[...] = jnp.zeros(raw_ref.shape, F32)

    @pl.when((pl.program_id(0) == 0) & (pl.program_id(1) == 0))
    def _():
        wcopy_ref[...] = wxbc_ref[...]
        hn_ref[...] = normed(x_ref)

    hn = hn_ref[...]

    for c in range(0, SSD_D_INNER, COL_CHUNK):
        acc = _dot(hn, wz_ref[:, c:c + COL_CHUNK])
        sz_ref[:, c:c + COL_CHUNK] = (acc * _sigmoid(acc)).astype(BF16)

    for c in range(0, SSD_CONV_DIM, COL_CHUNK):
        cs = slice(c, c + COL_CHUNK)
        acc = _dot(hn, wcopy_ref[:, cs])
        for j in range(COL_CHUNK // LANES):
            lanes = slice(c + j * LANES, c + (j + 1) * LANES)
            t = acc[:, j * LANES:(j + 1) * LANES]
            col = (c // LANES + j) % stage_ref.shape[0]
            stage_ref[col, 0:halo, :] = raw_ref[:, lanes]
            stage_ref[col, halo:halo + tm, :] = t
            conv = cb_ref[:, lanes] + cw_ref[SSD_CONV - 1:SSD_CONV, lanes] * t
            for k in range(1, SSD_CONV):
                conv = conv + (cw_ref[SSD_CONV - 1 - k:SSD_CONV - k, lanes] *
                               stage_ref[col, halo - k:halo - k + tm, :])
            xbc_ref[:, lanes] = (conv * _sigmoid(conv)).astype(BF16)
            raw_ref[:, lanes] = t[tm - halo:tm, :]

    dt_ref[...] = _softplus(_dot(hn, wdt_ref[...]) + dtb_ref[...])
    dtT_ref[...] = _softplus(_dot_nt(wdtT_ref[...], hn) + dtbT_ref[...])

    cos, sa, sb = cos_ref[...], sa_ref[...], sb_ref[...]
    half = ROT_DIM // 2
    for w_ref, o_ref in ((wq_ref, q_ref), (wk_ref, k_ref)):
        for c in range(0, ATT_WIDTH, COL_CHUNK):
            acc = _dot(hn, w_ref[:, c:c + COL_CHUNK])
            for j in range(0, COL_CHUNK, LANES):
                t = acc[:, j:j + LANES]
                nxt = pltpu.roll(t, LANES - half, axis=1)
                prv = pltpu.roll(t, half, axis=1)
                o_ref[:, c + j:c + j + LANES] = (t * cos + nxt * sa + prv * sb).astype(BF16)
    for c in range(0, ATT_WIDTH, COL_CHUNK):
        v_ref[:, c:c + COL_CHUNK] = _dot(hn, wv_ref[:, c:c + COL_CHUNK]).astype(BF16)

    for c in range(0, 2 * D_MODEL, COL_CHUNK):
        g_ref[:, c:c + COL_CHUNK] = _dot(hn, wg_ref[:, c:c + COL_CHUNK]).astype(BF16)

    hn_ref[...] = normed(xnext_ref)


def _rope_tables(s):
    half = ROT_DIM // 2
    pos = jnp.arange(s, dtype=F32)
    inv_freq = jnp.float32(ROPE_THETA) ** (-jnp.arange(0, ROT_DIM, 2, dtype=F32) / ROT_DIM)
    ang = pos[:, None] * inv_freq[None, :]
    ang = jnp.concatenate([ang, ang], axis=-1)
    cos, sin = jnp.cos(ang), jnp.sin(ang)
    pad = ATT_HEAD_DIM - ROT_DIM
    cos_t = jnp.concatenate([cos, jnp.ones((s, pad), F32)], axis=-1)
    sa_t = jnp.concatenate([-sin[:, :half], jnp.zeros((s, ATT_HEAD_DIM - half), F32)], axis=-1)
    sb_t = jnp.concatenate([jnp.zeros((s, half), F32), sin[:, half:], jnp.zeros((s, pad), F32)], axis=-1)
    reps = LANES // ATT_HEAD_DIM
    return tuple(jnp.tile(t, (1, reps)) for t in (cos_t, sa_t, sb_t))


def _in_projection(x, norm1_w, w_in, conv_w, conv_b, dt_bias):
    b, s, d = x.shape
    tm = ROW_TILE
    wz, wxbc, wdt, wq, wk, wv, wg = jnp.split(w_in.astype(BF16), IN_OFFSETS, axis=-1)
    cos_t, sa_t, sb_t = _rope_tables(s)
    row = lambda bi, ti: (bi, ti, 0)
    tab = lambda bi, ti: (ti, 0)

    def out(width, dtype=BF16):
        return jax.ShapeDtypeStruct((b, s, width), dtype)

    n_tiles = s // tm

    def next_row(bi, ti):
        n = jnp.minimum(bi * n_tiles + ti + 1, b * n_tiles - 1)
        return (n // n_tiles, n % n_tiles, 0)

    in_specs = [
        pl.BlockSpec((None, tm, d), row),
        pl.BlockSpec((None, tm, d), next_row),
        _resident((1, d)),
        _resident(wz.shape), _resident(wxbc.shape), _resident(wdt.shape), _resident((SSD_N_HEADS, d)),
        _resident(wq.shape), _resident(wk.shape), _resident(wv.shape), _resident(wg.shape),
        _resident((SSD_CONV, SSD_CONV_DIM)), _resident((1, SSD_CONV_DIM)),
        _resident((1, SSD_N_HEADS)), _resident((SSD_N_HEADS, 1)),
        pl.BlockSpec((tm, LANES), tab), pl.BlockSpec((tm, LANES), tab), pl.BlockSpec((tm, LANES), tab),
    ]
    out_specs = [
        pl.BlockSpec((None, tm, SSD_D_INNER), row),
        pl.BlockSpec((None, tm, SSD_CONV_DIM), row),
        pl.BlockSpec((None, tm, SSD_N_HEADS), row),
        pl.BlockSpec((None, SSD_N_HEADS, tm), lambda bi, ti: (bi, 0, ti)),
        pl.BlockSpec((None, tm, ATT_WIDTH), row),
        pl.BlockSpec((None, tm, ATT_WIDTH), row),
        pl.BlockSpec((None, tm, ATT_WIDTH), row),
        pl.BlockSpec((None, tm, 2 * D_MODEL), row),
    ]
    out_shape = [out(SSD_D_INNER), out(SSD_CONV_DIM), out(SSD_N_HEADS, F32),
                 jax.ShapeDtypeStruct((b, SSD_N_HEADS, s), F32),
                 out(ATT_WIDTH), out(ATT_WIDTH), out(ATT_WIDTH), out(2 * D_MODEL)]
    return pl.pallas_call(
        _inproj_kernel,
        grid=(b, s // tm),
        in_specs=in_specs,
        out_specs=out_specs,
        out_shape=out_shape,
        scratch_shapes=[pltpu.VMEM((SUBLANES, SSD_CONV_DIM), F32), pltpu.VMEM(wxbc.shape, BF16),
                        pltpu.VMEM((8, tm + SUBLANES, LANES), F32), pltpu.VMEM((tm, d), BF16)],
        compiler_params=pltpu.CompilerParams(
            dimension_semantics=("arbitrary", "arbitrary"), vmem_limit_bytes=VMEM_LIMIT_BIG),
        name="in_projection",
    )(x, x, norm1_w.reshape(1, d), wz, wxbc, wdt, wdt.T, wq, wk, wv, wg,
      conv_w, conv_b.reshape(1, -1),
      dt_bias.reshape(1, -1), dt_bias.reshape(-1, 1), cos_t, sa_t, sb_t)


def _ssd_kernel(xbc_ref, sz_ref, dt_ref, dtT_ref, alog_ref, alogT_ref, dsk_ref, nw_ref, e_ch_ref,
                y_ref, state_ref):
    l = SSD_CHUNK
    gw = SSD_GROUP_WIDTH
    hpg = SSD_HEADS_PER_GROUP
    p = SSD_HEAD_DIM

    @pl.when(pl.program_id(1) == 0)
    def _():
        state_ref[...] = jnp.zeros(state_ref.shape, F32)

    log2e = float(np.log2(np.e))
    dt = dt_ref[...]
    a = dt * (-log2e * jnp.exp(alog_ref[...]))
    aT = dtT_ref[...] * (-log2e * jnp.exp(alogT_ref[...]))

    ri = lax.broadcasted_iota(jnp.int32, (l, l), 0)
    ci = lax.broadcasted_iota(jnp.int32, (l, l), 1)
    lower = ri >= ci
    tri = jnp.where(lower, 1.0, 0.0).astype(BF16)
    triT = jnp.where(ri <= ci, 1.0, 0.0).astype(BF16)
    cs = sum(_dot(tri, piece) for piece in _split3(a))
    csT = sum(_dot(piece, triT) for piece in _split3(aT))
    dt_pieces = jnp.concatenate(_split3(dt), axis=1)
    cs_pieces = jnp.concatenate(_split3(cs), axis=1)

    lane_head = lax.broadcasted_iota(jnp.int32, (l, gw), 1) // p

    def group_steps(g):
        cols = slice(g * gw, (g + 1) * gw)
        v = {}

        def load_and_project():
            v["xg"] = xbc_ref[:, cols].astype(F32)
            v["bg"] = xbc_ref[:, SSD_D_INNER + g * SSD_D_STATE:SSD_D_INNER + (g + 1) * SSD_D_STATE]
            v["cg"] = xbc_ref[:, SSD_D_INNER + SSD_GN + g * SSD_D_STATE:
                              SSD_D_INNER + SSD_GN + (g + 1) * SSD_D_STATE]
            v["cb"] = _dot_nt(v["cg"], v["bg"])
            v["dt_g"] = _dot(dt_pieces, e_ch_ref[:, cols])
            v["cs_g"] = _dot(cs_pieces, e_ch_ref[:, cols])
            v["st"] = state_ref[g]
            v["y_off"] = _dot(v["cg"], v["st"].astype(BF16))

        def scale_inputs():
            tot_g = v["tot_g"] = v["cs_g"][l - 1:l, :]
            v["xdt"] = v["xg"] * v["dt_g"]
            v["xdec"] = (v["xdt"] * jnp.exp2(tot_g - v["cs_g"])).astype(BF16)
            v["y"] = v["xg"] * dsk_ref[:, cols] + v["y_off"] * jnp.exp2(v["cs_g"])

        def head_pair(r0):
            mix, rhs = [], []
            for r in (r0, r0 + 1):
                h = g * hpg + r
                decay = jnp.exp2(cs[:, h:h + 1] - csT[h:h + 1, :])
                mix.append(jnp.where(lower, v["cb"] * decay, 0.0).astype(BF16))
                rhs.append(jnp.where(lane_head == r, v["xdt"], 0.0).astype(BF16))
            v["y"] = v["y"] + _dot(jnp.concatenate(mix, axis=1), jnp.concatenate(rhs, axis=0))

        def update_state():
            state_ref[g] = v["st"] * jnp.exp2(v["tot_g"]) + _dot_tn(v["bg"], v["xdec"])

        def gate_and_norm():
            y = v["y"] * sz_ref[:, cols].astype(F32)
            ms = jnp.mean(y * y, axis=-1, keepdims=True)
            y_ref[:, cols] = (y * lax.rsqrt(ms + RMS_EPS) * nw_ref[:, cols]).astype(BF16)

        return ([load_and_project, scale_inputs] +
                [functools.partial(head_pair, r0) for r0 in range(0, hpg, 2)] +
                [update_state, gate_and_norm])

    pending = [group_steps(g) for g in range(SSD_N_GROUPS)]
    n_steps = len(pending[0])
    for t in range(SSD_N_GROUPS + n_steps - 1):
        for g in range(SSD_N_GROUPS):
            if 0 <= t - g < n_steps:
                pending[g][t - g]()


def _ssd(xbc, sz, dt, dtT, a_log, d_skip, norm_w):
    b, s, _ = xbc.shape
    l = SSD_CHUNK
    row = lambda bi, ci: (bi, ci, 0)
    piece_head = np.arange(3 * SSD_N_HEADS)[:, None] % SSD_N_HEADS
    e_ch = jnp.asarray(piece_head == np.arange(SSD_D_INNER)[None, :] // SSD_HEAD_DIM, BF16)
    return pl.pallas_call(
        _ssd_kernel,
        grid=(b, s // l),
        in_specs=[
            pl.BlockSpec((None, l, SSD_CONV_DIM), row),
            pl.BlockSpec((None, l, SSD_D_INNER), row),
            pl.BlockSpec((None, l, SSD_N_HEADS), row),
            pl.BlockSpec((None, SSD_N_HEADS, l), lambda bi, ci: (bi, 0, ci)),
            _resident((1, SSD_N_HEADS)), _resident((SSD_N_HEADS, 1)),
            _resident((1, SSD_D_INNER)), _resident((1, SSD_D_INNER)),
            _resident(e_ch.shape),
        ],
        out_specs=pl.BlockSpec((None, l, SSD_D_INNER), row),
        out_shape=jax.ShapeDtypeStruct((b, s, SSD_D_INNER), BF16),
        scratch_shapes=[pltpu.VMEM((SSD_N_GROUPS, SSD_D_STATE, SSD_GROUP_WIDTH), F32)],
        compiler_params=pltpu.CompilerParams(
            dimension_semantics=("arbitrary", "arbitrary"), vmem_limit_bytes=VMEM_LIMIT_SMALL),
        name="ssd_scan",
    )(xbc, sz, dt, dtT, a_log.reshape(1, -1), a_log.reshape(-1, 1),
      jnp.repeat(d_skip, SSD_HEAD_DIM).reshape(1, -1), norm_w.reshape(1, -1), e_ch)


VT_ROWS = ATT_HEAD_DIM + 16


def _moba_kernel(q_ref, k_ref, v_ref, o_ref, vt_ref, s_ref, p_ref):
    s = k_ref.shape[0]
    blk = MOBA_BLOCK
    nb = s // blk
    nbp = -(-nb // SUBLANES) * SUBLANES
    hd = ATT_HEAD_DIM

    for n in range(nb):
        vt = v_ref[n * blk:(n + 1) * blk, :].astype(F32).T
        for h in range(2):
            vt_ref[h, 0:hd, n * blk:(n + 1) * blk] = vt[h * hd:(h + 1) * hd].astype(BF16)
    vt_ref[:, hd:VT_ROWS, :] = jnp.ones((2, VT_ROWS - hd, s), BF16)

    arow = lax.broadcasted_iota(jnp.int32, (nbp, s), 0)
    acol = lax.broadcasted_iota(jnp.int32, (nbp, s), 1) // blk
    avg = jnp.where(arow == acol, 1.0 / blk, 0.0).astype(BF16)
    km = _dot(avg, k_ref[...])
    km_hi = km.astype(BF16)
    km_lo = (km - km_hi.astype(F32)).astype(BF16)

    head_row = lax.broadcasted_iota(jnp.int32, (LANES, blk), 0) // hd
    n_idx = lax.broadcasted_iota(jnp.int32, (nbp, blk), 0)
    causal_t = (lax.broadcasted_iota(jnp.int32, (blk, blk), 0) <=
                lax.broadcasted_iota(jnp.int32, (blk, blk), 1))
    scale = hd ** -0.5 * float(np.log2(np.e))

    n_slots = s_ref.shape[0]
    units = [(qi, h) for qi in range(nb) for h in range(2)]
    state = {}

    def score_steps(u):
        qi, h = units[u]
        slot = u % n_slots
        st = state[u] = {"biases": [], "m": None}

        def prologue():
            if h == 0:
                st["qt"] = q_ref[qi * blk:(qi + 1) * blk, :].astype(F32).T * scale
            else:
                st["qt"] = state[u - 1]["qt"]
            qt_h = st["qt_h"] = jnp.where(head_row == h, st["qt"], 0.0).astype(BF16)
            if qi > 0:
                gt = _dot(km_hi, qt_h) + _dot(km_lo, qt_h)
                rank = jnp.zeros((nbp, blk), jnp.int32)
                for m in range(qi):
                    gm = gt[m:m + 1, :]
                    beats = (gm > gt) | ((gm == gt) & (n_idx > m))
                    rank = rank + jnp.where(beats, 1, 0)
                st["bias_t"] = jnp.where(rank >= MOBA_TOPK, NEG_INF, 0.0)

        def block(n):
            rows = slice(n * blk, (n + 1) * blk)
            sc = _dot(k_ref[rows, :], st["qt_h"])
            if n == qi:
                sc = jnp.where(causal_t, sc, NEG_INF)
                b_n = jnp.zeros((1, blk), F32)
            else:
                b_n = st["bias_t"][n:n + 1, :]
            s_ref[slot, rows, :] = sc
            m_n = jnp.max(sc, axis=0, keepdims=True) + b_n
            st["m"] = m_n if st["m"] is None else jnp.maximum(st["m"], m_n)
            st["biases"].append(b_n)

        return [prologue] + [functools.partial(block, n) for n in range(qi + 1)]

    def prob_steps(u):
        qi, _ = units[u]
        slot = u % n_slots
        st = state[u]

        def block(n):
            rows = slice(n * blk, (n + 1) * blk)
            p_ref[slot, rows, :] = jnp.exp2(s_ref[slot, rows, :] - (st["m"] - st["biases"][n])).astype(BF16)

        return [functools.partial(block, n) for n in range(qi + 1)]

    def value_steps(u):
        qi, h = units[u]
        slot = u % n_slots
        st = state[u]
        st["acc"] = None

        def block(n):
            rows = slice(n * blk, (n + 1) * blk)
            part = _dot(vt_ref[h, :, rows], p_ref[slot, rows, :])
            st["acc"] = part if st["acc"] is None else st["acc"] + part

        def epilogue():
            acc = st["acc"]
            st["out"] = acc[0:hd] * (1.0 / acc[hd:hd + 1])
            if h == 1:
                pair = jnp.concatenate([state[u - 1]["out"], st["out"]], axis=0)
                o_ref[qi * blk:(qi + 1) * blk, :] = pair.T.astype(BF16)
                del state[u - 1]

        return [functools.partial(block, n) for n in range(qi + 1)] + [epilogue]

    for t in range(len(units) + 2):
        streams = [score_steps(t) if t < len(units) else [],
                   prob_steps(t - 1) if 0 <= t - 1 < len(units) else [],
                   value_steps(t - 2) if 0 <= t - 2 < len(units) else []]
        for i in range(max(len(stream) for stream in streams)):
            for stream in streams:
                if i < len(stream):
                    stream[i]()


def _moba(q, k, v):
    b, s, _ = q.shape
    spec = pl.BlockSpec((None, s, LANES), lambda bi, hi: (bi, 0, hi))
    return pl.pallas_call(
        _moba_kernel,
        grid=(b, ATT_N_HEADS // 2),
        in_specs=[spec, spec, spec],
        out_specs=spec,
        out_shape=jax.ShapeDtypeStruct((b, s, ATT_WIDTH), BF16),
        scratch_shapes=[pltpu.VMEM((2, VT_ROWS, s), BF16), pltpu.VMEM((4, s, MOBA_BLOCK), F32),
                        pltpu.VMEM((4, s, MOBA_BLOCK), BF16)],
        compiler_params=pltpu.CompilerParams(
            dimension_semantics=("arbitrary", "arbitrary"), vmem_limit_bytes=VMEM_LIMIT_SMALL),
        name="moba_attention",
    )(q, k, v)


def _mixer_kernel(x_ref, ys_ref, ya_ref, g_ref, bg_ref, wsp_ref, wap_ref, wo_ref, n2_ref, wup_ref, fcw_ref,
                  fcb_ref, wdn_ref, nf_ref, o_ref, up_ref):
    tm = x_ref.shape[0]
    halo = SUBLANES

    @pl.when(pl.program_id(1) == 0)
    def _():
        up_ref[:, 0:halo, :] = jnp.zeros((up_ref.shape[0], halo, LANES), F32)

    g_ssd = _sigmoid(g_ref[:, :D_MODEL].astype(F32) + bg_ref[:, :D_MODEL])
    g_att = _sigmoid(g_ref[:, D_MODEL:].astype(F32) + bg_ref[:, D_MODEL:])
    merged = g_ssd * _dot(ys_ref[...], wsp_ref[...]) + g_att * _dot(ya_ref[...], wap_ref[...])
    x1 = x_ref[...] + _dot(merged.astype(BF16), wo_ref[...])

    var = jnp.mean(x1 * x1, axis=-1, keepdims=True)
    h2 = (x1 * lax.rsqrt(var + RMS_EPS) * n2_ref[...]).astype(BF16)

    n_chunk = FFN_HIDDEN // 2
    y = x1
    for c in range(0, FFN_HIDDEN, n_chunk):
        cs = slice(c, c + n_chunk)
        up = _dot(h2, wup_ref[:, cs])
        convs = []
        for j in range(n_chunk // LANES):
            lanes = slice(c + j * LANES, c + (j + 1) * LANES)
            tile = (c // LANES) + j
            up_ref[tile, halo:halo + tm, :] = up[:, j * LANES:(j + 1) * LANES]
            conv = fcb_ref[:, lanes]
            for k in range(FFN_CONV):
                off = halo - (FFN_CONV - 1) + k
                conv = conv + fcw_ref[k:k + 1, lanes] * up_ref[tile, off:off + tm, :]
            up_ref[tile, 0:halo, :] = up_ref[tile, tm:tm + halo, :]
            convs.append(conv)
        conv = jnp.concatenate(convs, axis=1)
        gate = _dot(h2, wup_ref[:, FFN_HIDDEN + c:FFN_HIDDEN + c + n_chunk])
        act = (conv * _sigmoid(conv) * gate).astype(BF16)
        y = y + _dot(act, wdn_ref[cs, :])

    var = jnp.mean(y * y, axis=-1, keepdims=True)
    o_ref[...] = y * lax.rsqrt(var + RMS_EPS) * nf_ref[...]


def _mixer(x, y_ssd, y_att, gates, b_gate, w_ssd_proj, w_att_proj, w_out, norm2_w, w_ffn_up, ffn_conv_w, ffn_conv_b,
           w_ffn_down, norm_f_w):
    b, s, d = x.shape
    tm = ROW_TILE
    row = lambda bi, ti: (bi, ti, 0)
    wsp, wap, wo, wup, wdn = (w.astype(BF16) for w in (w_ssd_proj, w_att_proj, w_out, w_ffn_up, w_ffn_down))
    return pl.pallas_call(
        _mixer_kernel,
        grid=(b, s // tm),
        in_specs=[
            pl.BlockSpec((None, tm, d), row),
            pl.BlockSpec((None, tm, SSD_D_INNER), row),
            pl.BlockSpec((None, tm, ATT_WIDTH), row),
            pl.BlockSpec((None, tm, 2 * D_MODEL), row),
            _resident((1, 2 * D_MODEL)),
            _resident(wsp.shape), _resident(wap.shape), _resident(wo.shape), _resident((1, d)),
            _resident(wup.shape), _resident((FFN_CONV, FFN_HIDDEN)), _resident((1, FFN_HIDDEN)),
            _resident(wdn.shape), _resident((1, d)),
        ],
        out_specs=pl.BlockSpec((None, tm, d), row),
        out_shape=jax.ShapeDtypeStruct((b, s, d), F32),
        scratch_shapes=[pltpu.VMEM((FFN_HIDDEN // LANES, tm + SUBLANES, LANES), F32)],
        compiler_params=pltpu.CompilerParams(
            dimension_semantics=("arbitrary", "arbitrary"), vmem_limit_bytes=VMEM_LIMIT_BIG),
        name="merge_convffn",
    )(x, y_ssd, y_att, gates, b_gate.reshape(1, -1), wsp, wap, wo, norm2_w.reshape(1, d), wup, ffn_conv_w,
      ffn_conv_b.reshape(1, -1), wdn, norm_f_w.reshape(1, d))


def kernel(x, norm1_w, w_in, b_gate, ssd_conv_w, ssd_conv_b, ssd_dt_bias, ssd_a_log, ssd_d, ssd_norm_w,
           w_ssd_proj, w_att_proj, w_out, norm2_w, w_ffn_up, ffn_conv_w, ffn_conv_b, w_ffn_down, norm_f_w):
    assert norm1_w.shape[0] == 1, "single-layer block"
    sz, xbc, dt, dtT, q, k, v, gates = _in_projection(
        x, norm1_w[0], w_in[0], ssd_conv_w[0], ssd_conv_b[0], ssd_dt_bias[0])
    y_ssd = _ssd(xbc, sz, dt, dtT, ssd_a_log[0], ssd_d[0], ssd_norm_w[0])
    y_att = _moba(q, k, v)
    return _mixer(x, y_ssd, y_att, gates, b_gate[0], w_ssd_proj[0], w_att_proj[0], w_out[0], norm2_w[0], w_ffn_up[0],
                  ffn_conv_w[0], ffn_conv_b[0], w_ffn_down[0], norm_f_w)
```

```python
import functools

import jax
import jax.numpy as jnp
import numpy as np
from jax import lax
from jax.experimental import pallas as pl
from jax.experimental.pallas import tpu as pltpu

D_MODEL = 1024
RMS_EPS = 1e-6
NEG_INF = -1e30

SSD_D_INNER = 2048
SSD_HEAD_DIM = 64
SSD_N_HEADS = 32
SSD_N_GROUPS = 8
SSD_HEADS_PER_GROUP = SSD_N_HEADS // SSD_N_GROUPS
SSD_GROUP_WIDTH = SSD_D_INNER // SSD_N_GROUPS
SSD_D_STATE = 128
SSD_CONV = 4
SSD_CHUNK = 128
SSD_GN = SSD_N_GROUPS * SSD_D_STATE
SSD_CONV_DIM = SSD_D_INNER + 2 * SSD_GN

ATT_HEAD_DIM = 64
ATT_N_HEADS = 16
ATT_WIDTH = ATT_N_HEADS * ATT_HEAD_DIM
MOBA_BLOCK = 256
MOBA_TOPK = 3
ROPE_THETA = 500000.0
ROT_DIM = ATT_HEAD_DIM // 4

FFN_HIDDEN = 2816
FFN_CONV = 3

IN_SIZES = (SSD_D_INNER, SSD_CONV_DIM, SSD_N_HEADS, ATT_WIDTH, ATT_WIDTH, ATT_WIDTH, 2 * D_MODEL)
IN_OFFSETS = tuple(int(v) for v in np.cumsum(IN_SIZES)[:-1])

LANES = 128
SUBLANES = 8
VMEM_LIMIT_BIG = 56 * 1024 * 1024
VMEM_LIMIT_SMALL = 40 * 1024 * 1024

ROW_TILE = 256
COL_CHUNK = 512
CONV_STAGE_SLOTS = 2 * COL_CHUNK // LANES

BF16 = jnp.bfloat16
F32 = jnp.float32


def _dot(a, b):
    return jnp.dot(a, b, preferred_element_type=F32)


def _dot_nt(a, b):
    return lax.dot_general(a, b, (((1,), (1,)), ((), ())), preferred_element_type=F32)


def _dot_tn(a, b):
    return lax.dot_general(a, b, (((0,), (0,)), ((), ())), preferred_element_type=F32)


def _sigmoid(x):
    return 1.0 / (1.0 + jnp.exp2(x * float(-np.log2(np.e))))


def _softplus(x):
    return jnp.maximum(x, 0.0) + jnp.log1p(jnp.exp(-jnp.abs(x)))


def _split3(x):
    hi = x.astype(BF16)
    r = x - hi.astype(F32)
    mid = r.astype(BF16)
    lo = (r - mid.astype(F32)).astype(BF16)
    return hi, mid, lo


def _resident(shape):
    nd = len(shape)
    return pl.BlockSpec(shape, lambda *_: (0,) * nd, pipeline_mode=pl.Buffered(1))


def _inproj_kernel(x_ref, n1_ref, wz_ref, wxbc_ref, wdt_ref, wdtT_ref, wq_ref, wk_ref, wv_ref, wg_ref,
                   cw_ref, cb_ref, dtb_ref, dtbT_ref, cos_ref, sa_ref, sb_ref,
                   sz_ref, xbc_ref, dt_ref, dtT_ref, q_ref, k_ref, v_ref, g_ref, raw_ref, wcopy_ref, stage_ref):
    tm = x_ref.shape[0]
    halo = SUBLANES

    @pl.when(pl.program_id(1) == 0)
    def _():
        raw_ref[...] = jnp.zeros(raw_ref.shape, F32)

    @pl.when((pl.program_id(0) == 0) & (pl.program_id(1) == 0))
    def _():
        wcopy_ref[...] = wxbc_ref[...]

    xf = x_ref[...]
    var = jnp.mean(xf * xf, axis=-1, keepdims=True)
    hn = (xf * lax.rsqrt(var + RMS_EPS) * n1_ref[...]).astype(BF16)

    for c in range(0, SSD_D_INNER, COL_CHUNK):
        acc = _dot(hn, wz_ref[:, c:c + COL_CHUNK])
        sz_ref[:, c:c + COL_CHUNK] = (acc * _sigmoid(acc)).astype(BF16)

    for c in range(0, SSD_CONV_DIM, COL_CHUNK):
        cs = slice(c, c + COL_CHUNK)
        acc = _dot(hn, wcopy_ref[:, cs])
        for j in range(COL_CHUNK // LANES):
            lanes = slice(c + j * LANES, c + (j + 1) * LANES)
            t = acc[:, j * LANES:(j + 1) * LANES]
            col = (c // LANES + j) % stage_ref.shape[0]
            stage_ref[col, 0:halo, :] = raw_ref[:, lanes]
            stage_ref[col, halo:halo + tm, :] = t
            conv = cb_ref[:, lanes] + cw_ref[SSD_CONV - 1:SSD_CONV, lanes] * t
            for k in range(1, SSD_CONV):
                conv = conv + (cw_ref[SSD_CONV - 1 - k:SSD_CONV - k, lanes] *
                               stage_ref[col, halo - k:halo - k + tm, :])
            xbc_ref[:, lanes] = (conv * _sigmoid(conv)).astype(BF16)
            raw_ref[:, lanes] = t[tm - halo:tm, :]

    dt_ref[...] = _softplus(_dot(hn, wdt_ref[...]) + dtb_ref[...])
    dtT_ref[...] = _softplus(_dot_nt(wdtT_ref[...], hn) + dtbT_ref[...])

    cos, sa, sb = cos_ref[...], sa_ref[...], sb_ref[...]
    half = ROT_DIM // 2
    for w_ref, o_ref in ((wq_ref, q_ref), (wk_ref, k_ref)):
        for c in range(0, ATT_WIDTH, COL_CHUNK):
            acc = _dot(hn, w_ref[:, c:c + COL_CHUNK])
            for j in range(0, COL_CHUNK, LANES):
                t = acc[:, j:j + LANES]
                nxt = pltpu.roll(t, LANES - half, axis=1)
                prv = pltpu.roll(t, half, axis=1)
                o_ref[:, c + j:c + j + LANES] = (t * cos + nxt * sa + prv * sb).astype(BF16)
    for c in range(0, ATT_WIDTH, COL_CHUNK):
        v_ref[:, c:c + COL_CHUNK] = _dot(hn, wv_ref[:, c:c + COL_CHUNK]).astype(BF16)

    for c in range(0, 2 * D_MODEL, COL_CHUNK):
        g_ref[:, c:c + COL_CHUNK] = _dot(hn, wg_ref[:, c:c + COL_CHUNK]).astype(BF16)


def _rope_tables(s):
    half = ROT_DIM // 2
    pos = jnp.arange(s, dtype=F32)
    inv_freq = jnp.float32(ROPE_THETA) ** (-jnp.arange(0, ROT_DIM, 2, dtype=F32) / ROT_DIM)
    ang = pos[:, None] * inv_freq[None, :]
    ang = jnp.concatenate([ang, ang], axis=-1)
    cos, sin = jnp.cos(ang), jnp.sin(ang)
    pad = ATT_HEAD_DIM - ROT_DIM
    cos_t = jnp.concatenate([cos, jnp.ones((s, pad), F32)], axis=-1)
    sa_t = jnp.concatenate([-sin[:, :half], jnp.zeros((s, ATT_HEAD_DIM - half), F32)], axis=-1)
    sb_t = jnp.concatenate([jnp.zeros((s, half), F32), sin[:, half:], jnp.zeros((s, pad), F32)], axis=-1)
    reps = LANES // ATT_HEAD_DIM
    return tuple(jnp.tile(t, (1, reps)) for t in (cos_t, sa_t, sb_t))


def _in_projection(x, norm1_w, w_in, conv_w, conv_b, dt_bias):
    b, s, d = x.shape
    tm = ROW_TILE
    wz, wxbc, wdt, wq, wk, wv, wg = jnp.split(w_in.astype(BF16), IN_OFFSETS, axis=-1)
    cos_t, sa_t, sb_t = _rope_tables(s)
    row = lambda bi, ti: (bi, ti, 0)
    tab = lambda bi, ti: (ti, 0)

    def out(width, dtype=BF16):
        return jax.ShapeDtypeStruct((b, s, width), dtype)

    in_specs = [
        pl.BlockSpec((None, tm, d), row),
        _resident((1, d)),
        _resident(wz.shape), _resident(wxbc.shape), _resident(wdt.shape), _resident((SSD_N_HEADS, d)),
        _resident(wq.shape), _resident(wk.shape), _resident(wv.shape), _resident(wg.shape),
        _resident((SSD_CONV, SSD_CONV_DIM)), _resident((1, SSD_CONV_DIM)),
        _resident((1, SSD_N_HEADS)), _resident((SSD_N_HEADS, 1)),
        pl.BlockSpec((tm, LANES), tab), pl.BlockSpec((tm, LANES), tab), pl.BlockSpec((tm, LANES), tab),
    ]
    out_specs = [
        pl.BlockSpec((None, tm, SSD_D_INNER), row),
        pl.BlockSpec((None, tm, SSD_CONV_DIM), row),
        pl.BlockSpec((None, tm, SSD_N_HEADS), row),
        pl.BlockSpec((None, SSD_N_HEADS, tm), lambda bi, ti: (bi, 0, ti)),
        pl.BlockSpec((None, tm, ATT_WIDTH), row),
        pl.BlockSpec((None, tm, ATT_WIDTH), row),
        pl.BlockSpec((None, tm, ATT_WIDTH), row),
        pl.BlockSpec((None, tm, 2 * D_MODEL), row),
    ]
    out_shape = [out(SSD_D_INNER), out(SSD_CONV_DIM), out(SSD_N_HEADS, F32),
                 jax.ShapeDtypeStruct((b, SSD_N_HEADS, s), F32),
                 out(ATT_WIDTH), out(ATT_WIDTH), out(ATT_WIDTH), out(2 * D_MODEL)]
    return pl.pallas_call(
        _inproj_kernel,
        grid=(b, s // tm),
        in_specs=in_specs,
        out_specs=out_specs,
        out_shape=out_shape,
        scratch_shapes=[pltpu.VMEM((SUBLANES, SSD_CONV_DIM), F32), pltpu.VMEM(wxbc.shape, BF16),
                        pltpu.VMEM((CONV_STAGE_SLOTS, tm + SUBLANES, LANES), F32)],
        compiler_params=pltpu.CompilerParams(
            dimension_semantics=("arbitrary", "arbitrary"), vmem_limit_bytes=VMEM_LIMIT_BIG),
        name="in_projection",
    )(x, norm1_w.reshape(1, d), wz, wxbc, wdt, wdt.T, wq, wk, wv, wg,
      conv_w, conv_b.reshape(1, -1),
      dt_bias.reshape(1, -1), dt_bias.reshape(-1, 1), cos_t, sa_t, sb_t)


def _ssd_kernel(xbc_ref, sz_ref, dt_ref, dtT_ref, alog_ref, alogT_ref, dsk_ref, nw_ref, e_ch_ref,
                y_ref, state_ref):
    l = SSD_CHUNK
    gw = SSD_GROUP_WIDTH
    hpg = SSD_HEADS_PER_GROUP
    p = SSD_HEAD_DIM

    @pl.when(pl.program_id(1) == 0)
    def _():
        state_ref[...] = jnp.zeros(state_ref.shape, F32)

    log2e = float(np.log2(np.e))
    dt = dt_ref[...]
    a = dt * (-log2e * jnp.exp(alog_ref[...]))
    aT = dtT_ref[...] * (-log2e * jnp.exp(alogT_ref[...]))

    ri = lax.broadcasted_iota(jnp.int32, (l, l), 0)
    ci = lax.broadcasted_iota(jnp.int32, (l, l), 1)
    lower = ri >= ci
    tri = jnp.where(lower, 1.0, 0.0).astype(BF16)
    triT = jnp.where(ri <= ci, 1.0, 0.0).astype(BF16)
    cs = sum(_dot(tri, piece) for piece in _split3(a))
    csT = sum(_dot(piece, triT) for piece in _split3(aT))
    dt_pieces = jnp.concatenate(_split3(dt), axis=1)
    cs_pieces = jnp.concatenate(_split3(cs), axis=1)

    lane_head = lax.broadcasted_iota(jnp.int32, (l, gw), 1) // p

    def group_steps(g):
        cols = slice(g * gw, (g + 1) * gw)
        v = {}

        def load_and_project():
            v["xg"] = xbc_ref[:, cols].astype(F32)
            v["bg"] = xbc_ref[:, SSD_D_INNER + g * SSD_D_STATE:SSD_D_INNER + (g + 1) * SSD_D_STATE]
            v["cg"] = xbc_ref[:, SSD_D_INNER + SSD_GN + g * SSD_D_STATE:
                              SSD_D_INNER + SSD_GN + (g + 1) * SSD_D_STATE]
            v["cb"] = _dot_nt(v["cg"], v["bg"])
            v["dt_g"] = _dot(dt_pieces, e_ch_ref[:, cols])
            v["cs_g"] = _dot(cs_pieces, e_ch_ref[:, cols])
            v["st"] = state_ref[g]
            v["y_off"] = _dot(v["cg"], v["st"].astype(BF16))

        def scale_inputs():
            tot_g = v["tot_g"] = v["cs_g"][l - 1:l, :]
            v["xdt"] = v["xg"] * v["dt_g"]
            v["xdec"] = (v["xdt"] * jnp.exp2(tot_g - v["cs_g"])).astype(BF16)
            v["y"] = v["xg"] * dsk_ref[:, cols] + v["y_off"] * jnp.exp2(v["cs_g"])

        def head_pair(r0):
            mix, rhs = [], []
            for r in (r0, r0 + 1):
                h = g * hpg + r
                decay = jnp.exp2(cs[:, h:h + 1] - csT[h:h + 1, :])
                mix.append(jnp.where(lower, v["cb"] * decay, 0.0).astype(BF16))
                rhs.append(jnp.where(lane_head == r, v["xdt"], 0.0).astype(BF16))
            v["y"] = v["y"] + _dot(jnp.concatenate(mix, axis=1), jnp.concatenate(rhs, axis=0))

        def update_state():
            state_ref[g] = v["st"] * jnp.exp2(v["tot_g"]) + _dot_tn(v["bg"], v["xdec"])

        def gate_and_norm():
            y = v["y"] * sz_ref[:, cols].astype(F32)
            ms = jnp.mean(y * y, axis=-1, keepdims=True)
            y_ref[:, cols] = (y * lax.rsqrt(ms + RMS_EPS) * nw_ref[:, cols]).astype(BF16)

        return ([load_and_project, scale_inputs] +
                [functools.partial(head_pair, r0) for r0 in range(0, hpg, 2)] +
                [update_state, gate_and_norm])

    pending = [group_steps(g) for g in range(SSD_N_GROUPS)]
    n_steps = len(pending[0])
    for t in range(SSD_N_GROUPS + n_steps - 1):
        for g in range(SSD_N_GROUPS):
            if 0 <= t - g < n_steps:
                pending[g][t - g]()


def _ssd(xbc, sz, dt, dtT, a_log, d_skip, norm_w):
    b, s, _ = xbc.shape
    l = SSD_CHUNK
    row = lambda bi, ci: (bi, ci, 0)
    piece_head = np.arange(3 * SSD_N_HEADS)[:, None] % SSD_N_HEADS
    e_ch = jnp.asarray(piece_head == np.arange(SSD_D_INNER)[None, :] // SSD_HEAD_DIM, BF16)
    return pl.pallas_call(
        _ssd_kernel,
        grid=(b, s // l),
        in_specs=[
            pl.BlockSpec((None, l, SSD_CONV_DIM), row),
            pl.BlockSpec((None, l, SSD_D_INNER), row),
            pl.BlockSpec((None, l, SSD_N_HEADS), row),
            pl.BlockSpec((None, SSD_N_HEADS, l), lambda bi, ci: (bi, 0, ci)),
            _resident((1, SSD_N_HEADS)), _resident((SSD_N_HEADS, 1)),
            _resident((1, SSD_D_INNER)), _resident((1, SSD_D_INNER)),
            _resident(e_ch.shape),
        ],
        out_specs=pl.BlockSpec((None, l, SSD_D_INNER), row),
        out_shape=jax.ShapeDtypeStruct((b, s, SSD_D_INNER), BF16),
        scratch_shapes=[pltpu.VMEM((SSD_N_GROUPS, SSD_D_STATE, SSD_GROUP_WIDTH), F32)],
        compiler_params=pltpu.CompilerParams(
            dimension_semantics=("arbitrary", "arbitrary"), vmem_limit_bytes=VMEM_LIMIT_SMALL),
        name="ssd_scan",
    )(xbc, sz, dt, dtT, a_log.reshape(1, -1), a_log.reshape(-1, 1),
      jnp.repeat(d_skip, SSD_HEAD_DIM).reshape(1, -1), norm_w.reshape(1, -1), e_ch)


VT_ROWS = ATT_HEAD_DIM + 16
MOBA_SLOTS = 6


def _moba_kernel(q_ref, k_ref, v_ref, o_ref, vt_ref, s_ref, p_ref):
    s = k_ref.shape[0]
    blk = MOBA_BLOCK
    nb = s // blk
    nbp = -(-nb // SUBLANES) * SUBLANES
    hd = ATT_HEAD_DIM

    for n in range(nb):
        vt = v_ref[n * blk:(n + 1) * blk, :].astype(F32).T
        for h in range(2):
            vt_ref[h, 0:hd, n * blk:(n + 1) * blk] = vt[h * hd:(h + 1) * hd].astype(BF16)
    vt_ref[:, hd:VT_ROWS, :] = jnp.ones((2, VT_ROWS - hd, s), BF16)

    arow = lax.broadcasted_iota(jnp.int32, (nbp, s), 0)
    acol = lax.broadcasted_iota(jnp.int32, (nbp, s), 1) // blk
    avg = jnp.where(arow == acol, 1.0 / blk, 0.0).astype(BF16)
    km = _dot(avg, k_ref[...])
    km_hi = km.astype(BF16)
    km_lo = (km - km_hi.astype(F32)).astype(BF16)

    head_row = lax.broadcasted_iota(jnp.int32, (LANES, blk), 0) // hd
    n_idx = lax.broadcasted_iota(jnp.int32, (nbp, blk), 0)
    causal_t = (lax.broadcasted_iota(jnp.int32, (blk, blk), 0) <=
                lax.broadcasted_iota(jnp.int32, (blk, blk), 1))
    scale = hd ** -0.5 * float(np.log2(np.e))

    n_slots = s_ref.shape[0]
    units = [(qi, h) for qi in range(nb) for h in range(2)]
    state = {}

    def score_steps(u):
        qi, h = units[u]
        slot = u % n_slots
        st = state[u] = {"biases": [], "m": None}

        def prologue():
            if h == 0:
                st["qt"] = q_ref[qi * blk:(qi + 1) * blk, :].astype(F32).T * scale
            else:
                st["qt"] = state[u - 1]["qt"]
            qt_h = st["qt_h"] = jnp.where(head_row == h, st["qt"], 0.0).astype(BF16)
            if qi > 0:
                gt = _dot(km_hi, qt_h) + _dot(km_lo, qt_h)
                rank = jnp.zeros((nbp, blk), jnp.int32)
                for m in range(qi):
                    gm = gt[m:m + 1, :]
                    beats = (gm > gt) | ((gm == gt) & (n_idx > m))
                    rank = rank + jnp.where(beats, 1, 0)
                st["bias_t"] = jnp.where(rank >= MOBA_TOPK, NEG_INF, 0.0)

        def block(n):
            rows = slice(n * blk, (n + 1) * blk)
            sc = _dot(k_ref[rows, :], st["qt_h"])
            if n == qi:
                sc = jnp.where(causal_t, sc, NEG_INF)
                b_n = jnp.zeros((1, blk), F32)
            else:
                b_n = st["bias_t"][n:n + 1, :]
            s_ref[slot, rows, :] = sc
            m_n = jnp.max(sc, axis=0, keepdims=True) + b_n
            st["m"] = m_n if st["m"] is None else jnp.maximum(st["m"], m_n)
            st["biases"].append(b_n)

        return [prologue] + [functools.partial(block, n) for n in range(qi + 1)]

    def prob_steps(u):
        qi, _ = units[u]
        slot = u % n_slots
        st = state[u]

        def block(n):
            rows = slice(n * blk, (n + 1) * blk)
            p_ref[slot, rows, :] = jnp.exp2(s_ref[slot, rows, :] - (st["m"] - st["biases"][n])).astype(BF16)

        return [functools.partial(block, n) for n in range(qi + 1)]

    def value_steps(u):
        qi, h = units[u]
        slot = u % n_slots
        st = state[u]
        st["acc"] = None

        def block(n):
            rows = slice(n * blk, (n + 1) * blk)
            part = _dot(vt_ref[h, :, rows], p_ref[slot, rows, :])
            st["acc"] = part if st["acc"] is None else st["acc"] + part

        def epilogue():
            acc = st["acc"]
            st["out"] = acc[0:hd] * (1.0 / acc[hd:hd + 1])
            if h == 1:
                pair = jnp.concatenate([state[u - 1]["out"], st["out"]], axis=0)
                o_ref[qi * blk:(qi + 1) * blk, :] = pair.T.astype(BF16)
                del state[u - 1]

        return [functools.partial(block, n) for n in range(qi + 1)] + [epilogue]

    width = MOBA_SLOTS // 3

    def stage(make_steps, first):
        return [make_steps(u) for u in range(first, first + width) if 0 <= u < len(units)]

    for t in range(0, len(units) + 2 * width, width):
        streams = (stage(score_steps, t) + stage(prob_steps, t - width) +
                   stage(value_steps, t - 2 * width))
        for i in range(max(len(stream) for stream in streams)):
            for stream in streams:
                if i < len(stream):
                    stream[i]()


def _moba(q, k, v):
    b, s, _ = q.shape
    spec = pl.BlockSpec((None, s, LANES), lambda bi, hi: (bi, 0, hi))
    return pl.pallas_call(
        _moba_kernel,
        grid=(b, ATT_N_HEADS // 2),
        in_specs=[spec, spec, spec],
        out_specs=spec,
        out_shape=jax.ShapeDtypeStruct((b, s, ATT_WIDTH), BF16),
        scratch_shapes=[pltpu.VMEM((2, VT_ROWS, s), BF16), pltpu.VMEM((MOBA_SLOTS, s, MOBA_BLOCK), F32),
                        pltpu.VMEM((MOBA_SLOTS, s, MOBA_BLOCK), BF16)],
        compiler_params=pltpu.CompilerParams(
            dimension_semantics=("arbitrary", "arbitrary"), vmem_limit_bytes=VMEM_LIMIT_SMALL),
        name="moba_attention",
    )(q, k, v)


def _mixer_kernel(x_ref, ys_ref, ya_ref, g_ref, bg_ref, wsp_ref, wap_ref, wo_ref, n2_ref, wup_ref, fcw_ref,
                  fcb_ref, wdn_ref, nf_ref, o_ref, up_ref):
    tm = x_ref.shape[0]
    halo = SUBLANES

    @pl.when(pl.program_id(1) == 0)
    def _():
        up_ref[:, 0:halo, :] = jnp.zeros((up_ref.shape[0], halo, LANES), F32)

    g_ssd = _sigmoid(g_ref[:, :D_MODEL].astype(F32) + bg_ref[:, :D_MODEL])
    g_att = _sigmoid(g_ref[:, D_MODEL:].astype(F32) + bg_ref[:, D_MODEL:])
    merged = g_ssd * _dot(ys_ref[...], wsp_ref[...]) + g_att * _dot(ya_ref[...], wap_ref[...])
    x1 = x_ref[...] + _dot(merged.astype(BF16), wo_ref[...])

    var = jnp.mean(x1 * x1, axis=-1, keepdims=True)
    h2 = (x1 * lax.rsqrt(var + RMS_EPS) * n2_ref[...]).astype(BF16)

    n_chunk = FFN_HIDDEN // 2
    y = x1
    for c in range(0, FFN_HIDDEN, n_chunk):
        cs = slice(c, c + n_chunk)
        up = _dot(h2, wup_ref[:, cs])
        convs = []
        for j in range(n_chunk // LANES):
            lanes = slice(c + j * LANES, c + (j + 1) * LANES)
            tile = (c // LANES) + j
            up_ref[tile, halo:halo + tm, :] = up[:, j * LANES:(j + 1) * LANES]
            conv = fcb_ref[:, lanes]
            for k in range(FFN_CONV):
                off = halo - (FFN_CONV - 1) + k
                conv = conv + fcw_ref[k:k + 1, lanes] * up_ref[tile, off:off + tm, :]
            up_ref[tile, 0:halo, :] = up_ref[tile, tm:tm + halo, :]
            convs.append(conv)
        conv = jnp.concatenate(convs, axis=1)
        gate = _dot(h2, wup_ref[:, FFN_HIDDEN + c:FFN_HIDDEN + c + n_chunk])
        act = (conv * _sigmoid(conv) * gate).astype(BF16)
        y = y + _dot(act, wdn_ref[cs, :])

    var = jnp.mean(y * y, axis=-1, keepdims=True)
    o_ref[...] = y * lax.rsqrt(var + RMS_EPS) * nf_ref[...]


def _mixer(x, y_ssd, y_att, gates, b_gate, w_ssd_proj, w_att_proj, w_out, norm2_w, w_ffn_up, ffn_conv_w, ffn_conv_b,
           w_ffn_down, norm_f_w):
    b, s, d = x.shape
    tm = ROW_TILE
    row = lambda bi, ti: (bi, ti, 0)
    wsp, wap, wo, wup, wdn = (w.astype(BF16) for w in (w_ssd_proj, w_att_proj, w_out, w_ffn_up, w_ffn_down))
    return pl.pallas_call(
        _mixer_kernel,
        grid=(b, s // tm),
        in_specs=[
            pl.BlockSpec((None, tm, d), row),
            pl.BlockSpec((None, tm, SSD_D_INNER), row),
            pl.BlockSpec((None, tm, ATT_WIDTH), row),
            pl.BlockSpec((None, tm, 2 * D_MODEL), row),
            _resident((1, 2 * D_MODEL)),
            _resident(wsp.shape), _resident(wap.shape), _resident(wo.shape), _resident((1, d)),
            _resident(wup.shape), _resident((FFN_CONV, FFN_HIDDEN)), _resident((1, FFN_HIDDEN)),
            _resident(wdn.shape), _resident((1, d)),
        ],
        out_specs=pl.BlockSpec((None, tm, d), row),
        out_shape=jax.ShapeDtypeStruct((b, s, d), F32),
        scratch_shapes=[pltpu.VMEM((FFN_HIDDEN // LANES, tm + SUBLANES, LANES), F32)],
        compiler_params=pltpu.CompilerParams(
            dimension_semantics=("arbitrary", "arbitrary"), vmem_limit_bytes=VMEM_LIMIT_BIG),
        name="merge_convffn",
    )(x, y_ssd, y_att, gates, b_gate.reshape(1, -1), wsp, wap, wo, norm2_w.reshape(1, d), wup, ffn_conv_w,
      ffn_conv_b.reshape(1, -1), wdn, norm_f_w.reshape(1, d))


def kernel(x, norm1_w, w_in, b_gate, ssd_conv_w, ssd_conv_b, ssd_dt_bias, ssd_a_log, ssd_d, ssd_norm_w,
           w_ssd_proj, w_att_proj, w_out, norm2_w, w_ffn_up, ffn_conv_w, ffn_conv_b, w_ffn_down, norm_f_w):
    assert norm1_w.shape[0] == 1, "single-layer block"
    sz, xbc, dt, dtT, q, k, v, gates = _in_projection(
        x, norm1_w[0], w_in[0], ssd_conv_w[0], ssd_conv_b[0], ssd_dt_bias[0])
    y_ssd = _ssd(xbc, sz, dt, dtT, ssd_a_log[0], ssd_d[0], ssd_norm_w[0])
    y_att = _moba(q, k, v)
    return _mixer(x, y_ssd, y_att, gates, b_gate[0], w_ssd_proj[0], w_att_proj[0], w_out[0], norm2_w[0], w_ffn_up[0],
                  ffn_conv_w[0], ffn_conv_b[0], w_ffn_down[0], norm_f_w)
```

```python
import functools

import jax
import jax.numpy as jnp
import numpy as np
from jax import lax
from jax.experimental import pallas as pl
from jax.experimental.pallas import tpu as pltpu

D_MODEL = 1024
RMS_EPS = 1e-6
NEG_INF = -1e30

SSD_D_INNER = 2048
SSD_HEAD_DIM = 64
SSD_N_HEADS = 32
SSD_N_GROUPS = 8
SSD_HEADS_PER_GROUP = SSD_N_HEADS // SSD_N_GROUPS
SSD_GROUP_WIDTH = SSD_D_INNER // SSD_N_GROUPS
SSD_D_STATE = 128
SSD_CONV = 4
SSD_CHUNK = 128
SSD_GN = SSD_N_GROUPS * SSD_D_STATE
SSD_CONV_DIM = SSD_D_INNER + 2 * SSD_GN

ATT_HEAD_DIM = 64
ATT_N_HEADS = 16
ATT_WIDTH = ATT_N_HEADS * ATT_HEAD_DIM
MOBA_BLOCK = 256
MOBA_TOPK = 3
ROPE_THETA = 500000.0
ROT_DIM = ATT_HEAD_DIM // 4

FFN_HIDDEN = 2816
FFN_CONV = 3

IN_SIZES = (SSD_D_INNER, SSD_CONV_DIM, SSD_N_HEADS, ATT_WIDTH, ATT_WIDTH, ATT_WIDTH, 2 * D_MODEL)
IN_OFFSETS = tuple(int(v) for v in np.cumsum(IN_SIZES)[:-1])

LANES = 128
SUBLANES = 8
VMEM_LIMIT_BIG = 56 * 1024 * 1024
VMEM_LIMIT_SMALL = 40 * 1024 * 1024

ROW_TILE = 256
COL_CHUNK = 512
CONV_STAGE_SLOTS = 2 * COL_CHUNK // LANES

BF16 = jnp.bfloat16
F32 = jnp.float32


def _dot(a, b):
    return jnp.dot(a, b, preferred_element_type=F32)


def _dot_nt(a, b):
    return lax.dot_general(a, b, (((1,), (1,)), ((), ())), preferred_element_type=F32)


def _dot_tn(a, b):
    return lax.dot_general(a, b, (((0,), (0,)), ((), ())), preferred_element_type=F32)


def _sigmoid(x):
    return 1.0 / (1.0 + jnp.exp2(x * float(-np.log2(np.e))))


def _softplus(x):
    return jnp.maximum(x, 0.0) + jnp.log1p(jnp.exp(-jnp.abs(x)))


def _split3(x):
    hi = x.astype(BF16)
    r = x - hi.astype(F32)
    mid = r.astype(BF16)
    lo = (r - mid.astype(F32)).astype(BF16)
    return hi, mid, lo


def _resident(shape):
    nd = len(shape)
    return pl.BlockSpec(shape, lambda *_: (0,) * nd, pipeline_mode=pl.Buffered(1))


def _inproj_kernel(x_ref, n1_ref, wz_ref, wxbc_ref, wdt_ref, wdtT_ref, wq_ref, wk_ref, wv_ref, wg_ref,
                   cw_ref, cb_ref, dtb_ref, dtbT_ref, cos_ref, sa_ref, sb_ref,
                   sz_ref, xbc_ref, dt_ref, dtT_ref, q_ref, k_ref, v_ref, g_ref, raw_ref, wcopy_ref, stage_ref):
    tm = x_ref.shape[0]
    halo = SUBLANES

    @pl.when(pl.program_id(1) == 0)
    def _():
        raw_ref[...] = jnp.zeros(raw_ref.shape, F32)

    @pl.when((pl.program_id(0) == 0) & (pl.program_id(1) == 0))
    def _():
        wcopy_ref[...] = wxbc_ref[...]

    xf = x_ref[...]
    var = jnp.mean(xf * xf, axis=-1, keepdims=True)
    hn = (xf * lax.rsqrt(var + RMS_EPS) * n1_ref[...]).astype(BF16)

    for c in range(0, SSD_D_INNER, COL_CHUNK):
        acc = _dot(hn, wz_ref[:, c:c + COL_CHUNK])
        sz_ref[:, c:c + COL_CHUNK] = (acc * _sigmoid(acc)).astype(BF16)

    for c in range(0, SSD_CONV_DIM, COL_CHUNK):
        cs = slice(c, c + COL_CHUNK)
        acc = _dot(hn, wcopy_ref[:, cs])
        for j in range(COL_CHUNK // LANES):
            lanes = slice(c + j * LANES, c + (j + 1) * LANES)
            t = acc[:, j * LANES:(j + 1) * LANES]
            col = (c // LANES + j) % stage_ref.shape[0]
            stage_ref[col, 0:halo, :] = raw_ref[:, lanes]
            stage_ref[col, halo:halo + tm, :] = t
            conv = cb_ref[:, lanes] + cw_ref[SSD_CONV - 1:SSD_CONV, lanes] * t
            for k in range(1, SSD_CONV):
                conv = conv + (cw_ref[SSD_CONV - 1 - k:SSD_CONV - k, lanes] *
                               stage_ref[col, halo - k:halo - k + tm, :])
            xbc_ref[:, lanes] = (conv * _sigmoid(conv)).astype(BF16)
            raw_ref[:, lanes] = t[tm - halo:tm, :]

    dt_ref[...] = _softplus(_dot(hn, wdt_ref[...]) + dtb_ref[...])
    dtT_ref[...] = _softplus(_dot_nt(wdtT_ref[...], hn) + dtbT_ref[...])

    cos, sa, sb = cos_ref[...], sa_ref[...], sb_ref[...]
    half = ROT_DIM // 2
    for w_ref, o_ref in ((wq_ref, q_ref), (wk_ref, k_ref)):
        for c in range(0, ATT_WIDTH, COL_CHUNK):
            acc = _dot(hn, w_ref[:, c:c + COL_CHUNK])
            for j in range(0, COL_CHUNK, LANES):
                t = acc[:, j:j + LANES]
                nxt = pltpu.roll(t, LANES - half, axis=1)
                prv = pltpu.roll(t, half, axis=1)
                o_ref[:, c + j:c + j + LANES] = (t * cos + nxt * sa + prv * sb).astype(BF16)
    for c in range(0, ATT_WIDTH, COL_CHUNK):
        v_ref[:, c:c + COL_CHUNK] = _dot(hn, wv_ref[:, c:c + COL_CHUNK]).astype(BF16)

    for c in range(0, 2 * D_MODEL, COL_CHUNK):
        g_ref[:, c:c + COL_CHUNK] = _dot(hn, wg_ref[:, c:c + COL_CHUNK]).astype(BF16)


def _rope_tables(s):
    half = ROT_DIM // 2
    pos = jnp.arange(s, dtype=F32)
    inv_freq = jnp.float32(ROPE_THETA) ** (-jnp.arange(0, ROT_DIM, 2, dtype=F32) / ROT_DIM)
    ang = pos[:, None] * inv_freq[None, :]
    ang = jnp.concatenate([ang, ang], axis=-1)
    cos, sin = jnp.cos(ang), jnp.sin(ang)
    pad = ATT_HEAD_DIM - ROT_DIM
    cos_t = jnp.concatenate([cos, jnp.ones((s, pad), F32)], axis=-1)
    sa_t = jnp.concatenate([-sin[:, :half], jnp.zeros((s, ATT_HEAD_DIM - half), F32)], axis=-1)
    sb_t = jnp.concatenate([jnp.zeros((s, half), F32), sin[:, half:], jnp.zeros((s, pad), F32)], axis=-1)
    reps = LANES // ATT_HEAD_DIM
    return tuple(jnp.tile(t, (1, reps)) for t in (cos_t, sa_t, sb_t))


def _in_projection(x, norm1_w, w_in, conv_w, conv_b, dt_bias):
    b, s, d = x.shape
    tm = ROW_TILE
    wz, wxbc, wdt, wq, wk, wv, wg = jnp.split(w_in.astype(BF16), IN_OFFSETS, axis=-1)
    cos_t, sa_t, sb_t = _rope_tables(s)
    row = lambda bi, ti: (bi, ti, 0)
    tab = lambda bi, ti: (ti, 0)

    def out(width, dtype=BF16):
        return jax.ShapeDtypeStruct((b, s, width), dtype)

    in_specs = [
        pl.BlockSpec((None, tm, d), row),
        _resident((1, d)),
        _resident(wz.shape), _resident(wxbc.shape), _resident(wdt.shape), _resident((SSD_N_HEADS, d)),
        _resident(wq.shape), _resident(wk.shape), _resident(wv.shape), _resident(wg.shape),
        _resident((SSD_CONV, SSD_CONV_DIM)), _resident((1, SSD_CONV_DIM)),
        _resident((1, SSD_N_HEADS)), _resident((SSD_N_HEADS, 1)),
        pl.BlockSpec((tm, LANES), tab), pl.BlockSpec((tm, LANES), tab), pl.BlockSpec((tm, LANES), tab),
    ]
    out_specs = [
        pl.BlockSpec((None, tm, SSD_D_INNER), row),
        pl.BlockSpec((None, tm, SSD_CONV_DIM), row),
        pl.BlockSpec((None, tm, SSD_N_HEADS), row),
        pl.BlockSpec((None, SSD_N_HEADS, tm), lambda bi, ti: (bi, 0, ti)),
        pl.BlockSpec((None, tm, ATT_WIDTH), row),
        pl.BlockSpec((None, tm, ATT_WIDTH), row),
        pl.BlockSpec((None, tm, ATT_WIDTH), row),
        pl.BlockSpec((None, tm, 2 * D_MODEL), row),
    ]
    out_shape = [out(SSD_D_INNER), out(SSD_CONV_DIM), out(SSD_N_HEADS, F32),
                 jax.ShapeDtypeStruct((b, SSD_N_HEADS, s), F32),
                 out(ATT_WIDTH), out(ATT_WIDTH), out(ATT_WIDTH), out(2 * D_MODEL)]
    return pl.pallas_call(
        _inproj_kernel,
        grid=(b, s // tm),
        in_specs=in_specs,
        out_specs=out_specs,
        out_shape=out_shape,
        scratch_shapes=[pltpu.VMEM((SUBLANES, SSD_CONV_DIM), F32), pltpu.VMEM(wxbc.shape, BF16),
                        pltpu.VMEM((CONV_STAGE_SLOTS, tm + SUBLANES, LANES), F32)],
        compiler_params=pltpu.CompilerParams(
            dimension_semantics=("arbitrary", "arbitrary"), vmem_limit_bytes=VMEM_LIMIT_BIG),
        name="in_projection",
    )(x, norm1_w.reshape(1, d), wz, wxbc, wdt, wdt.T, wq, wk, wv, wg,
      conv_w, conv_b.reshape(1, -1),
      dt_bias.reshape(1, -1), dt_bias.reshape(-1, 1), cos_t, sa_t, sb_t)


def _ssd_kernel(xbc_ref, sz_ref, dt_ref, dtT_ref, alog_ref, alogT_ref, dsk_ref, nw_ref, e_ch_ref,
                y_ref, state_ref):
    l = SSD_CHUNK
    gw = SSD_GROUP_WIDTH
    hpg = SSD_HEADS_PER_GROUP
    p = SSD_HEAD_DIM

    @pl.when(pl.program_id(1) == 0)
    def _():
        state_ref[...] = jnp.zeros(state_ref.shape, F32)

    log2e = float(np.log2(np.e))
    dt = dt_ref[...]
    a = dt * (-log2e * jnp.exp(alog_ref[...]))
    aT = dtT_ref[...] * (-log2e * jnp.exp(alogT_ref[...]))

    ri = lax.broadcasted_iota(jnp.int32, (l, l), 0)
    ci = lax.broadcasted_iota(jnp.int32, (l, l), 1)
    lower = ri >= ci
    tri = jnp.where(lower, 1.0, 0.0).astype(BF16)
    triT = jnp.where(ri <= ci, 1.0, 0.0).astype(BF16)
    cs = sum(_dot(tri, piece) for piece in _split3(a))
    csT = sum(_dot(piece, triT) for piece in _split3(aT))
    dt_pieces = jnp.concatenate(_split3(dt), axis=1)
    cs_pieces = jnp.concatenate(_split3(cs), axis=1)

    lane_head = lax.broadcasted_iota(jnp.int32, (l, gw), 1) // p

    def group_steps(g):
        cols = slice(g * gw, (g + 1) * gw)
        v = {}

        def load_and_project():
            v["xg"] = xbc_ref[:, cols].astype(F32)
            v["bg"] = xbc_ref[:, SSD_D_INNER + g * SSD_D_STATE:SSD_D_INNER + (g + 1) * SSD_D_STATE]
            v["cg"] = xbc_ref[:, SSD_D_INNER + SSD_GN + g * SSD_D_STATE:
                              SSD_D_INNER + SSD_GN + (g + 1) * SSD_D_STATE]
            v["cb"] = _dot_nt(v["cg"], v["bg"])
            v["dt_g"] = _dot(dt_pieces, e_ch_ref[:, cols])
            v["cs_g"] = _dot(cs_pieces, e_ch_ref[:, cols])
            v["st"] = state_ref[g]
            v["y_off"] = _dot(v["cg"], v["st"].astype(BF16))

        def scale_inputs():
            tot_g = v["tot_g"] = v["cs_g"][l - 1:l, :]
            v["xdt"] = v["xg"] * v["dt_g"]
            v["xdec"] = (v["xdt"] * jnp.exp2(tot_g - v["cs_g"])).astype(BF16)
            v["y"] = v["xg"] * dsk_ref[:, cols] + v["y_off"] * jnp.exp2(v["cs_g"])

        def head_pair(r0):
            mix, rhs = [], []
            for r in (r0, r0 + 1):
                h = g * hpg + r
                decay = jnp.exp2(cs[:, h:h + 1] - csT[h:h + 1, :])
                mix.append(jnp.where(lower, v["cb"] * decay, 0.0).astype(BF16))
                rhs.append(jnp.where(lane_head == r, v["xdt"], 0.0).astype(BF16))
            v["y"] = v["y"] + _dot(jnp.concatenate(mix, axis=1), jnp.concatenate(rhs, axis=0))

        def update_state():
            state_ref[g] = v["st"] * jnp.exp2(v["tot_g"]) + _dot_tn(v["bg"], v["xdec"])

        def gate_and_norm():
            y = v["y"] * sz_ref[:, cols].astype(F32)
            ms = jnp.mean(y * y, axis=-1, keepdims=True)
            y_ref[:, cols] = (y * lax.rsqrt(ms + RMS_EPS) * nw_ref[:, cols]).astype(BF16)

        return ([load_and_project, scale_inputs] +
                [functools.partial(head_pair, r0) for r0 in range(0, hpg, 2)] +
                [update_state, gate_and_norm])

    pending = [group_steps(g) for g in range(SSD_N_GROUPS)]
    n_steps = len(pending[0])
    for t in range(SSD_N_GROUPS + n_steps - 1):
        for g in range(SSD_N_GROUPS):
            if 0 <= t - g < n_steps:
                pending[g][t - g]()


def _ssd(xbc, sz, dt, dtT, a_log, d_skip, norm_w):
    b, s, _ = xbc.shape
    l = SSD_CHUNK
    row = lambda bi, ci: (bi, ci, 0)
    piece_head = np.arange(3 * SSD_N_HEADS)[:, None] % SSD_N_HEADS
    e_ch = jnp.asarray(piece_head == np.arange(SSD_D_INNER)[None, :] // SSD_HEAD_DIM, BF16)
    return pl.pallas_call(
        _ssd_kernel,
        grid=(b, s // l),
        in_specs=[
            pl.BlockSpec((None, l, SSD_CONV_DIM), row),
            pl.BlockSpec((None, l, SSD_D_INNER), row),
            pl.BlockSpec((None, l, SSD_N_HEADS), row),
            pl.BlockSpec((None, SSD_N_HEADS, l), lambda bi, ci: (bi, 0, ci)),
            _resident((1, SSD_N_HEADS)), _resident((SSD_N_HEADS, 1)),
            _resident((1, SSD_D_INNER)), _resident((1, SSD_D_INNER)),
            _resident(e_ch.shape),
        ],
        out_specs=pl.BlockSpec((None, l, SSD_D_INNER), row),
        out_shape=jax.ShapeDtypeStruct((b, s, SSD_D_INNER), BF16),
        scratch_shapes=[pltpu.VMEM((SSD_N_GROUPS, SSD_D_STATE, SSD_GROUP_WIDTH), F32)],
        compiler_params=pltpu.CompilerParams(
            dimension_semantics=("arbitrary", "arbitrary"), vmem_limit_bytes=VMEM_LIMIT_SMALL),
        name="ssd_scan",
    )(xbc, sz, dt, dtT, a_log.reshape(1, -1), a_log.reshape(-1, 1),
      jnp.repeat(d_skip, SSD_HEAD_DIM).reshape(1, -1), norm_w.reshape(1, -1), e_ch)


VT_ROWS = ATT_HEAD_DIM + 16
MOBA_SLOTS = 6


def _moba_kernel(q_ref, k_ref, v_ref, o_ref, vt_ref, s_ref, p_ref):
    s = k_ref.shape[0]
    blk = MOBA_BLOCK
    nb = s // blk
    nbp = -(-nb // SUBLANES) * SUBLANES
    hd = ATT_HEAD_DIM

    for n in range(nb):
        vt = v_ref[n * blk:(n + 1) * blk, :].astype(F32).T
        for h in range(2):
            vt_ref[h, 0:hd, n * blk:(n + 1) * blk] = vt[h * hd:(h + 1) * hd].astype(BF16)
    vt_ref[:, hd:VT_ROWS, :] = jnp.ones((2, VT_ROWS - hd, s), BF16)

    arow = lax.broadcasted_iota(jnp.int32, (nbp, s), 0)
    acol = lax.broadcasted_iota(jnp.int32, (nbp, s), 1) // blk
    avg = jnp.where(arow == acol, 1.0 / blk, 0.0).astype(BF16)
    km = _dot(avg, k_ref[...])
    km_hi = km.astype(BF16)
    km_lo = (km - km_hi.astype(F32)).astype(BF16)

    head_row = lax.broadcasted_iota(jnp.int32, (LANES, blk), 0) // hd
    n_idx = lax.broadcasted_iota(jnp.int32, (nbp, blk), 0)
    causal_t = (lax.broadcasted_iota(jnp.int32, (blk, blk), 0) <=
                lax.broadcasted_iota(jnp.int32, (blk, blk), 1))
    scale = hd ** -0.5 * float(np.log2(np.e))

    n_slots = s_ref.shape[0]
    units = [(qi, h) for qi in range(nb) for h in range(2)]
    state = {}

    def score_steps(u):
        qi, h = units[u]
        slot = u % n_slots
        st = state[u] = {"biases": [], "m": None}

        def prologue():
            if h == 0:
                st["qt"] = q_ref[qi * blk:(qi + 1) * blk, :].astype(F32).T * scale
            else:
                st["qt"] = state[u - 1]["qt"]
            qt_h = st["qt_h"] = jnp.where(head_row == h, st["qt"], 0.0).astype(BF16)
            if qi > 0:
                gt = _dot(km_hi, qt_h) + _dot(km_lo, qt_h)
                rank = jnp.zeros((nbp, blk), jnp.int32)
                for m in range(qi):
                    gm = gt[m:m + 1, :]
                    beats = (gm > gt) | ((gm == gt) & (n_idx > m))
                    rank = rank + jnp.where(beats, 1, 0)
                st["bias_t"] = jnp.where(rank >= MOBA_TOPK, NEG_INF, 0.0)

        def block(n):
            rows = slice(n * blk, (n + 1) * blk)
            sc = _dot(k_ref[rows, :], st["qt_h"])
            if n == qi:
                sc = jnp.where(causal_t, sc, NEG_INF)
                b_n = jnp.zeros((1, blk), F32)
            else:
                b_n = st["bias_t"][n:n + 1, :]
            s_ref[slot, rows, :] = sc
            m_n = jnp.max(sc, axis=0, keepdims=True) + b_n
            st["m"] = m_n if st["m"] is None else jnp.maximum(st["m"], m_n)
            st["biases"].append(b_n)

        return [prologue] + [functools.partial(block, n) for n in range(qi + 1)]

    def prob_steps(u):
        qi, _ = units[u]
        slot = u % n_slots
        st = state[u]

        def block(n):
            rows = slice(n * blk, (n + 1) * blk)
            p_ref[slot, rows, :] = jnp.exp2(s_ref[slot, rows, :] - (st["m"] - st["biases"][n])).astype(BF16)

        return [functools.partial(block, n) for n in range(qi + 1)]

    def value_steps(u):
        qi, h = units[u]
        slot = u % n_slots
        st = state[u]
        st["acc"] = None

        def block(n):
            rows = slice(n * blk, (n + 1) * blk)
            part = _dot(vt_ref[h, :, rows], p_ref[slot, rows, :])
            st["acc"] = part if st["acc"] is None else st["acc"] + part

        def epilogue():
            acc = st["acc"]
            st["out"] = acc[0:hd] * (1.0 / acc[hd:hd + 1])
            if h == 1:
                pair = jnp.concatenate([state[u - 1]["out"], st["out"]], axis=0)
                o_ref[qi * blk:(qi + 1) * blk, :] = pair.T.astype(BF16)
                del state[u - 1]

        return [functools.partial(block, n) for n in range(qi + 1)] + [epilogue]

    width = MOBA_SLOTS // 3

    def stage(make_steps, first):
        return [make_steps(u) for u in range(first, first + width) if 0 <= u < len(units)]

    for t in range(0, len(units) + 2 * width, width):
        streams = (stage(score_steps, t) + stage(prob_steps, t - width) +
                   stage(value_steps, t - 2 * width))
        for i in range(max(len(stream) for stream in streams)):
            for stream in streams:
                if i < len(stream):
                    stream[i]()


def _moba(q, k, v):
    b, s, _ = q.shape
    spec = pl.BlockSpec((None, s, LANES), lambda bi, hi: (bi, 0, hi))
    return pl.pallas_call(
        _moba_kernel,
        grid=(b, ATT_N_HEADS // 2),
        in_specs=[spec, spec, spec],
        out_specs=spec,
        out_shape=jax.ShapeDtypeStruct((b, s, ATT_WIDTH), BF16),
        scratch_shapes=[pltpu.VMEM((2, VT_ROWS, s), BF16), pltpu.VMEM((MOBA_SLOTS, s, MOBA_BLOCK), F32),
                        pltpu.VMEM((MOBA_SLOTS, s, MOBA_BLOCK), BF16)],
        compiler_params=pltpu.CompilerParams(
            dimension_semantics=("arbitrary", "arbitrary"), vmem_limit_bytes=VMEM_LIMIT_SMALL),
        name="moba_attention",
    )(q, k, v)


def _mixer_kernel(x_ref, ys_ref, ya_ref, g_ref, bg_ref, wsp_ref, wap_ref, wo_ref, n2_ref, wup_ref, fcw_ref,
                  fcb_ref, wdn_ref, nf_ref, o_ref, up_ref):
    tm = x_ref.shape[0]
    halo = SUBLANES

    @pl.when(pl.program_id(1) == 0)
    def _():
        up_ref[:, 0:halo, :] = jnp.zeros((up_ref.shape[0], halo, LANES), F32)

    g_ssd = _sigmoid(g_ref[:, :D_MODEL].astype(F32) + bg_ref[:, :D_MODEL])
    g_att = _sigmoid(g_ref[:, D_MODEL:].astype(F32) + bg_ref[:, D_MODEL:])
    merged = g_ssd * _dot(ys_ref[...], wsp_ref[...]) + g_att * _dot(ya_ref[...], wap_ref[...])
    x1 = x_ref[...] + _dot(merged.astype(BF16), wo_ref[...])

    var = jnp.mean(x1 * x1, axis=-1, keepdims=True)
    h2 = (x1 * lax.rsqrt(var + RMS_EPS) * n2_ref[...]).astype(BF16)

    up = _dot(h2, wup_ref[:, 0:FFN_HIDDEN])
    convs = []
    for tile in range(FFN_HIDDEN // LANES):
        lanes = slice(tile * LANES, (tile + 1) * LANES)
        up_ref[tile, halo:halo + tm, :] = up[:, lanes]
        conv = fcb_ref[:, lanes]
        for k in range(FFN_CONV):
            off = halo - (FFN_CONV - 1) + k
            conv = conv + fcw_ref[k:k + 1, lanes] * up_ref[tile, off:off + tm, :]
        up_ref[tile, 0:halo, :] = up_ref[tile, tm:tm + halo, :]
        convs.append(conv)
    conv = jnp.concatenate(convs, axis=1)
    gate = _dot(h2, wup_ref[:, FFN_HIDDEN:2 * FFN_HIDDEN])
    act = (conv * _sigmoid(conv) * gate).astype(BF16)
    y = x1 + _dot(act, wdn_ref[...])

    var = jnp.mean(y * y, axis=-1, keepdims=True)
    o_ref[...] = y * lax.rsqrt(var + RMS_EPS) * nf_ref[...]


def _mixer(x, y_ssd, y_att, gates, b_gate, w_ssd_proj, w_att_proj, w_out, norm2_w, w_ffn_up, ffn_conv_w, ffn_conv_b,
           w_ffn_down, norm_f_w):
    b, s, d = x.shape
    tm = ROW_TILE
    row = lambda bi, ti: (bi, ti, 0)
    wsp, wap, wo, wup, wdn = (w.astype(BF16) for w in (w_ssd_proj, w_att_proj, w_out, w_ffn_up, w_ffn_down))
    return pl.pallas_call(
        _mixer_kernel,
        grid=(b, s // tm),
        in_specs=[
            pl.BlockSpec((None, tm, d), row),
            pl.BlockSpec((None, tm, SSD_D_INNER), row),
            pl.BlockSpec((None, tm, ATT_WIDTH), row),
            pl.BlockSpec((None, tm, 2 * D_MODEL), row),
            _resident((1, 2 * D_MODEL)),
            _resident(wsp.shape), _resident(wap.shape), _resident(wo.shape), _resident((1, d)),
            _resident(wup.shape), _resident((FFN_CONV, FFN_HIDDEN)), _resident((1, FFN_HIDDEN)),
            _resident(wdn.shape), _resident((1, d)),
        ],
        out_specs=pl.BlockSpec((None, tm, d), row),
        out_shape=jax.ShapeDtypeStruct((b, s, d), F32),
        scratch_shapes=[pltpu.VMEM((FFN_HIDDEN // LANES, tm + SUBLANES, LANES), F32)],
        compiler_params=pltpu.CompilerParams(
            dimension_semantics=("arbitrary", "arbitrary"), vmem_limit_bytes=VMEM_LIMIT_BIG),
        name="merge_convffn",
    )(x, y_ssd, y_att, gates, b_gate.reshape(1, -1), wsp, wap, wo, norm2_w.reshape(1, d), wup, ffn_conv_w,
      ffn_conv_b.reshape(1, -1), wdn, norm_f_w.reshape(1, d))


def kernel(x, norm1_w, w_in, b_gate, ssd_conv_w, ssd_conv_b, ssd_dt_bias, ssd_a_log, ssd_d, ssd_norm_w,
           w_ssd_proj, w_att_proj, w_out, norm2_w, w_ffn_up, ffn_conv_w, ffn_conv_b, w_ffn_down, norm_f_w):
    assert norm1_w.shape[0] == 1, "single-layer block"
    sz, xbc, dt, dtT, q, k, v, gates = _in_projection(
        x, norm1_w[0], w_in[0], ssd_conv_w[0], ssd_conv_b[0], ssd_dt_bias[0])
    y_ssd = _ssd(xbc, sz, dt, dtT, ssd_a_log[0], ssd_d[0], ssd_norm_w[0])
    y_att = _moba(q, k, v)
    return _mixer(x, y_ssd, y_att, gates, b_gate[0], w_ssd_proj[0], w_att_proj[0], w_out[0], norm2_w[0], w_ffn_up[0],
                  ffn_conv_w[0], ffn_conv_b[0], w_ffn_down[0], norm_f_w)
```

```python
import functools

import jax
import jax.numpy as jnp
import numpy as np
from jax import lax
from jax.experimental import pallas as pl
from jax.experimental.pallas import tpu as pltpu

D_MODEL = 1024
RMS_EPS = 1e-6
NEG_INF = -1e30

SSD_D_INNER = 2048
SSD_HEAD_DIM = 64
SSD_N_HEADS = 32
SSD_N_GROUPS = 8
SSD_HEADS_PER_GROUP = SSD_N_HEADS // SSD_N_GROUPS
SSD_GROUP_WIDTH = SSD_D_INNER // SSD_N_GROUPS
SSD_D_STATE = 128
SSD_CONV = 4
SSD_CHUNK = 128
SSD_GN = SSD_N_GROUPS * SSD_D_STATE
SSD_CONV_DIM = SSD_D_INNER + 2 * SSD_GN

ATT_HEAD_DIM = 64
ATT_N_HEADS = 16
ATT_WIDTH = ATT_N_HEADS * ATT_HEAD_DIM
MOBA_BLOCK = 256
MOBA_TOPK = 3
ROPE_THETA = 500000.0
ROT_DIM = ATT_HEAD_DIM // 4

FFN_HIDDEN = 2816
FFN_CONV = 3

IN_SIZES = (SSD_D_INNER, SSD_CONV_DIM, SSD_N_HEADS, ATT_WIDTH, ATT_WIDTH, ATT_WIDTH, 2 * D_MODEL)
IN_OFFSETS = tuple(int(v) for v in np.cumsum(IN_SIZES)[:-1])

LANES = 128
SUBLANES = 8
VMEM_LIMIT_BIG = 56 * 1024 * 1024
VMEM_LIMIT_SMALL = 40 * 1024 * 1024

ROW_TILE = 256
COL_CHUNK = 512
CONV_STAGE_SLOTS = 2 * COL_CHUNK // LANES

BF16 = jnp.bfloat16
F32 = jnp.float32


def _dot(a, b):
    return jnp.dot(a, b, preferred_element_type=F32)


def _dot_nt(a, b):
    return lax.dot_general(a, b, (((1,), (1,)), ((), ())), preferred_element_type=F32)


def _dot_tn(a, b):
    return lax.dot_general(a, b, (((0,), (0,)), ((), ())), preferred_element_type=F32)


def _sigmoid(x):
    return 1.0 / (1.0 + jnp.exp2(x * float(-np.log2(np.e))))


def _softplus(x):
    return jnp.maximum(x, 0.0) + jnp.log1p(jnp.exp(-jnp.abs(x)))


def _split3(x):
    hi = x.astype(BF16)
    r = x - hi.astype(F32)
    mid = r.astype(BF16)
    lo = (r - mid.astype(F32)).astype(BF16)
    return hi, mid, lo


def _resident(shape):
    nd = len(shape)
    return pl.BlockSpec(shape, lambda *_: (0,) * nd, pipeline_mode=pl.Buffered(1))


def _inproj_kernel(x_ref, n1_ref, wz_ref, wxbc_ref, wdt_ref, wdtT_ref, wq_ref, wk_ref, wv_ref, wg_ref,
                   cw_ref, cb_ref, dtb_ref, dtbT_ref, cos_ref, sa_ref, sb_ref,
                   sz_ref, xbc_ref, dt_ref, dtT_ref, q_ref, k_ref, v_ref, g_ref, raw_ref, wcopy_ref, stage_ref):
    tm = x_ref.shape[0]
    halo = SUBLANES

    @pl.when(pl.program_id(1) == 0)
    def _():
        raw_ref[...] = jnp.zeros(raw_ref.shape, F32)

    @pl.when((pl.program_id(0) == 0) & (pl.program_id(1) == 0))
    def _():
        wcopy_ref[...] = wxbc_ref[...]

    xf = x_ref[...]
    var = jnp.mean(xf * xf, axis=-1, keepdims=True)
    hn = (xf * lax.rsqrt(var + RMS_EPS) * n1_ref[...]).astype(BF16)

    for c in range(0, SSD_D_INNER, COL_CHUNK):
        acc = _dot(hn, wz_ref[:, c:c + COL_CHUNK])
        sz_ref[:, c:c + COL_CHUNK] = (acc * _sigmoid(acc)).astype(BF16)

    for c in range(0, SSD_CONV_DIM, COL_CHUNK):
        cs = slice(c, c + COL_CHUNK)
        acc = _dot(hn, wcopy_ref[:, cs])
        for j in range(COL_CHUNK // LANES):
            lanes = slice(c + j * LANES, c + (j + 1) * LANES)
            t = acc[:, j * LANES:(j + 1) * LANES]
            col = (c // LANES + j) % stage_ref.shape[0]
            stage_ref[col, 0:halo, :] = raw_ref[:, lanes]
            stage_ref[col, halo:halo + tm, :] = t
            conv = cb_ref[:, lanes] + cw_ref[SSD_CONV - 1:SSD_CONV, lanes] * t
            for k in range(1, SSD_CONV):
                conv = conv + (cw_ref[SSD_CONV - 1 - k:SSD_CONV - k, lanes] *
                               stage_ref[col, halo - k:halo - k + tm, :])
            xbc_ref[:, lanes] = (conv * _sigmoid(conv)).astype(BF16)
            raw_ref[:, lanes] = t[tm - halo:tm, :]

    dt_ref[...] = _softplus(_dot(hn, wdt_ref[...]) + dtb_ref[...])
    dtT_ref[...] = _softplus(_dot_nt(wdtT_ref[...], hn) + dtbT_ref[...])

    cos, sa, sb = cos_ref[...], sa_ref[...], sb_ref[...]
    half = ROT_DIM // 2
    for w_ref, o_ref in ((wq_ref, q_ref), (wk_ref, k_ref)):
        for c in range(0, ATT_WIDTH, COL_CHUNK):
            acc = _dot(hn, w_ref[:, c:c + COL_CHUNK])
            for j in range(0, COL_CHUNK, LANES):
                t = acc[:, j:j + LANES]
                nxt = pltpu.roll(t, LANES - half, axis=1)
                prv = pltpu.roll(t, half, axis=1)
                o_ref[:, c + j:c + j + LANES] = (t * cos + nxt * sa + prv * sb).astype(BF16)
    for c in range(0, ATT_WIDTH, COL_CHUNK):
        acc = _dot(hn, wv_ref[:, c:c + COL_CHUNK])
        for j in range(0, COL_CHUNK, LANES):
            v_ref[c + j:c + j + LANES, :] = acc[:, j:j + LANES].T.astype(BF16)

    for c in range(0, 2 * D_MODEL, COL_CHUNK):
        g_ref[:, c:c + COL_CHUNK] = _dot(hn, wg_ref[:, c:c + COL_CHUNK]).astype(BF16)


def _rope_tables(s):
    half = ROT_DIM // 2
    pos = jnp.arange(s, dtype=F32)
    inv_freq = jnp.float32(ROPE_THETA) ** (-jnp.arange(0, ROT_DIM, 2, dtype=F32) / ROT_DIM)
    ang = pos[:, None] * inv_freq[None, :]
    ang = jnp.concatenate([ang, ang], axis=-1)
    cos, sin = jnp.cos(ang), jnp.sin(ang)
    pad = ATT_HEAD_DIM - ROT_DIM
    cos_t = jnp.concatenate([cos, jnp.ones((s, pad), F32)], axis=-1)
    sa_t = jnp.concatenate([-sin[:, :half], jnp.zeros((s, ATT_HEAD_DIM - half), F32)], axis=-1)
    sb_t = jnp.concatenate([jnp.zeros((s, half), F32), sin[:, half:], jnp.zeros((s, pad), F32)], axis=-1)
    reps = LANES // ATT_HEAD_DIM
    return tuple(jnp.tile(t, (1, reps)) for t in (cos_t, sa_t, sb_t))


def _in_projection(x, norm1_w, w_in, conv_w, conv_b, dt_bias):
    b, s, d = x.shape
    tm = ROW_TILE
    wz, wxbc, wdt, wq, wk, wv, wg = jnp.split(w_in.astype(BF16), IN_OFFSETS, axis=-1)
    cos_t, sa_t, sb_t = _rope_tables(s)
    row = lambda bi, ti: (bi, ti, 0)
    tab = lambda bi, ti: (ti, 0)

    def out(width, dtype=BF16):
        return jax.ShapeDtypeStruct((b, s, width), dtype)

    in_specs = [
        pl.BlockSpec((None, tm, d), row),
        _resident((1, d)),
        _resident(wz.shape), _resident(wxbc.shape), _resident(wdt.shape), _resident((SSD_N_HEADS, d)),
        _resident(wq.shape), _resident(wk.shape), _resident(wv.shape), _resident(wg.shape),
        _resident((SSD_CONV, SSD_CONV_DIM)), _resident((1, SSD_CONV_DIM)),
        _resident((1, SSD_N_HEADS)), _resident((SSD_N_HEADS, 1)),
        pl.BlockSpec((tm, LANES), tab), pl.BlockSpec((tm, LANES), tab), pl.BlockSpec((tm, LANES), tab),
    ]
    out_specs = [
        pl.BlockSpec((None, tm, SSD_D_INNER), row),
        pl.BlockSpec((None, tm, SSD_CONV_DIM), row),
        pl.BlockSpec((None, tm, SSD_N_HEADS), row),
        pl.BlockSpec((None, SSD_N_HEADS, tm), lambda bi, ti: (bi, 0, ti)),
        pl.BlockSpec((None, tm, ATT_WIDTH), row),
        pl.BlockSpec((None, tm, ATT_WIDTH), row),
        pl.BlockSpec((None, ATT_WIDTH, tm), lambda bi, ti: (bi, 0, ti)),
        pl.BlockSpec((None, tm, 2 * D_MODEL), row),
    ]
    out_shape = [out(SSD_D_INNER), out(SSD_CONV_DIM), out(SSD_N_HEADS, F32),
                 jax.ShapeDtypeStruct((b, SSD_N_HEADS, s), F32),
                 out(ATT_WIDTH), out(ATT_WIDTH), jax.ShapeDtypeStruct((b, ATT_WIDTH, s), BF16),
                 out(2 * D_MODEL)]
    return pl.pallas_call(
        _inproj_kernel,
        grid=(b, s // tm),
        in_specs=in_specs,
        out_specs=out_specs,
        out_shape=out_shape,
        scratch_shapes=[pltpu.VMEM((SUBLANES, SSD_CONV_DIM), F32), pltpu.VMEM(wxbc.shape, BF16),
                        pltpu.VMEM((CONV_STAGE_SLOTS, tm + SUBLANES, LANES), F32)],
        compiler_params=pltpu.CompilerParams(
            dimension_semantics=("arbitrary", "arbitrary"), vmem_limit_bytes=VMEM_LIMIT_BIG),
        name="in_projection",
    )(x, norm1_w.reshape(1, d), wz, wxbc, wdt, wdt.T, wq, wk, wv, wg,
      conv_w, conv_b.reshape(1, -1),
      dt_bias.reshape(1, -1), dt_bias.reshape(-1, 1), cos_t, sa_t, sb_t)


def _ssd_kernel(xbc_ref, sz_ref, dt_ref, dtT_ref, alog_ref, alogT_ref, dsk_ref, nw_ref, e_ch_ref,
                y_ref, state_ref):
    l = SSD_CHUNK
    gw = SSD_GROUP_WIDTH
    hpg = SSD_HEADS_PER_GROUP
    p = SSD_HEAD_DIM

    @pl.when(pl.program_id(1) == 0)
    def _():
        state_ref[...] = jnp.zeros(state_ref.shape, F32)

    log2e = float(np.log2(np.e))
    dt = dt_ref[...]
    a = dt * (-log2e * jnp.exp(alog_ref[...]))
    aT = dtT_ref[...] * (-log2e * jnp.exp(alogT_ref[...]))

    ri = lax.broadcasted_iota(jnp.int32, (l, l), 0)
    ci = lax.broadcasted_iota(jnp.int32, (l, l), 1)
    lower = ri >= ci
    tri = jnp.where(lower, 1.0, 0.0).astype(BF16)
    triT = jnp.where(ri <= ci, 1.0, 0.0).astype(BF16)
    cs = sum(_dot(tri, piece) for piece in _split3(a))
    csT = sum(_dot(piece, triT) for piece in _split3(aT))
    dt_pieces = jnp.concatenate(_split3(dt), axis=1)
    cs_pieces = jnp.concatenate(_split3(cs), axis=1)

    lane_head = lax.broadcasted_iota(jnp.int32, (l, gw), 1) // p

    def group_steps(g):
        cols = slice(g * gw, (g + 1) * gw)
        v = {}

        def load_and_project():
            v["xg"] = xbc_ref[:, cols].astype(F32)
            v["bg"] = xbc_ref[:, SSD_D_INNER + g * SSD_D_STATE:SSD_D_INNER + (g + 1) * SSD_D_STATE]
            v["cg"] = xbc_ref[:, SSD_D_INNER + SSD_GN + g * SSD_D_STATE:
                              SSD_D_INNER + SSD_GN + (g + 1) * SSD_D_STATE]
            v["cb"] = _dot_nt(v["cg"], v["bg"])
            v["dt_g"] = _dot(dt_pieces, e_ch_ref[:, cols])
            v["cs_g"] = _dot(cs_pieces, e_ch_ref[:, cols])
            v["st"] = state_ref[g]
            v["y_off"] = _dot(v["cg"], v["st"].astype(BF16))

        def scale_inputs():
            tot_g = v["tot_g"] = v["cs_g"][l - 1:l, :]
            v["xdt"] = v["xg"] * v["dt_g"]
            v["xdec"] = (v["xdt"] * jnp.exp2(tot_g - v["cs_g"])).astype(BF16)
            v["y"] = v["xg"] * dsk_ref[:, cols] + v["y_off"] * jnp.exp2(v["cs_g"])

        def head_pair(r0):
            mix, rhs = [], []
            for r in (r0, r0 + 1):
                h = g * hpg + r
                decay = jnp.exp2(cs[:, h:h + 1] - csT[h:h + 1, :])
                mix.append(jnp.where(lower, v["cb"] * decay, 0.0).astype(BF16))
                rhs.append(jnp.where(lane_head == r, v["xdt"], 0.0).astype(BF16))
            v["y"] = v["y"] + _dot(jnp.concatenate(mix, axis=1), jnp.concatenate(rhs, axis=0))

        def update_state():
            state_ref[g] = v["st"] * jnp.exp2(v["tot_g"]) + _dot_tn(v["bg"], v["xdec"])

        def gate_and_norm():
            y = v["y"] * sz_ref[:, cols].astype(F32)
            ms = jnp.mean(y * y, axis=-1, keepdims=True)
            y_ref[:, cols] = (y * lax.rsqrt(ms + RMS_EPS) * nw_ref[:, cols]).astype(BF16)

        return ([load_and_project, scale_inputs] +
                [functools.partial(head_pair, r0) for r0 in range(0, hpg, 2)] +
                [update_state, gate_and_norm])

    pending = [group_steps(g) for g in range(SSD_N_GROUPS)]
    n_steps = len(pending[0])
    for t in range(SSD_N_GROUPS + n_steps - 1):
        for g in range(SSD_N_GROUPS):
            if 0 <= t - g < n_steps:
                pending[g][t - g]()


def _ssd(xbc, sz, dt, dtT, a_log, d_skip, norm_w):
    b, s, _ = xbc.shape
    l = SSD_CHUNK
    row = lambda bi, ci: (bi, ci, 0)
    piece_head = np.arange(3 * SSD_N_HEADS)[:, None] % SSD_N_HEADS
    e_ch = jnp.asarray(piece_head == np.arange(SSD_D_INNER)[None, :] // SSD_HEAD_DIM, BF16)
    return pl.pallas_call(
        _ssd_kernel,
        grid=(b, s // l),
        in_specs=[
            pl.BlockSpec((None, l, SSD_CONV_DIM), row),
            pl.BlockSpec((None, l, SSD_D_INNER), row),
            pl.BlockSpec((None, l, SSD_N_HEADS), row),
            pl.BlockSpec((None, SSD_N_HEADS, l), lambda bi, ci: (bi, 0, ci)),
            _resident((1, SSD_N_HEADS)), _resident((SSD_N_HEADS, 1)),
            _resident((1, SSD_D_INNER)), _resident((1, SSD_D_INNER)),
            _resident(e_ch.shape),
        ],
        out_specs=pl.BlockSpec((None, l, SSD_D_INNER), row),
        out_shape=jax.ShapeDtypeStruct((b, s, SSD_D_INNER), BF16),
        scratch_shapes=[pltpu.VMEM((SSD_N_GROUPS, SSD_D_STATE, SSD_GROUP_WIDTH), F32)],
        compiler_params=pltpu.CompilerParams(
            dimension_semantics=("arbitrary", "arbitrary"), vmem_limit_bytes=VMEM_LIMIT_SMALL),
        name="ssd_scan",
    )(xbc, sz, dt, dtT, a_log.reshape(1, -1), a_log.reshape(-1, 1),
      jnp.repeat(d_skip, SSD_HEAD_DIM).reshape(1, -1), norm_w.reshape(1, -1), e_ch)


VT_ROWS = ATT_HEAD_DIM + 16
MOBA_SLOTS = 6


def _moba_kernel(q_ref, k_ref, v_ref, o_ref, vt_ref, s_ref, p_ref):
    s = k_ref.shape[0]
    blk = MOBA_BLOCK
    nb = s // blk
    nbp = -(-nb // SUBLANES) * SUBLANES
    hd = ATT_HEAD_DIM

    for h in range(2):
        vt_ref[h, 0:hd, :] = v_ref[h * hd:(h + 1) * hd, :]
    vt_ref[:, hd:VT_ROWS, :] = jnp.ones((2, VT_ROWS - hd, s), BF16)

    arow = lax.broadcasted_iota(jnp.int32, (nbp, s), 0)
    acol = lax.broadcasted_iota(jnp.int32, (nbp, s), 1) // blk
    avg = jnp.where(arow == acol, 1.0 / blk, 0.0).astype(BF16)
    km = _dot(avg, k_ref[...])
    km_hi = km.astype(BF16)
    km_lo = (km - km_hi.astype(F32)).astype(BF16)

    head_row = lax.broadcasted_iota(jnp.int32, (LANES, blk), 0) // hd
    n_idx = lax.broadcasted_iota(jnp.int32, (nbp, blk), 0)
    causal_t = (lax.broadcasted_iota(jnp.int32, (blk, blk), 0) <=
                lax.broadcasted_iota(jnp.int32, (blk, blk), 1))
    scale = hd ** -0.5 * float(np.log2(np.e))

    n_slots = s_ref.shape[0]
    units = [(qi, h) for qi in range(nb) for h in range(2)]
    state = {}

    def score_steps(u):
        qi, h = units[u]
        slot = u % n_slots
        st = state[u] = {"biases": [], "m": None}

        def prologue():
            if h == 0:
                st["qt"] = q_ref[qi * blk:(qi + 1) * blk, :].astype(F32).T * scale
            else:
                st["qt"] = state[u - 1]["qt"]
            qt_h = st["qt_h"] = jnp.where(head_row == h, st["qt"], 0.0).astype(BF16)
            if qi > 0:
                gt = _dot(km_hi, qt_h) + _dot(km_lo, qt_h)
                rank = jnp.zeros((nbp, blk), jnp.int32)
                for m in range(qi):
                    gm = gt[m:m + 1, :]
                    beats = (gm > gt) | ((gm == gt) & (n_idx > m))
                    rank = rank + jnp.where(beats, 1, 0)
                st["bias_t"] = jnp.where(rank >= MOBA_TOPK, NEG_INF, 0.0)

        def block(n):
            rows = slice(n * blk, (n + 1) * blk)
            sc = _dot(k_ref[rows, :], st["qt_h"])
            if n == qi:
                sc = jnp.where(causal_t, sc, NEG_INF)
                b_n = jnp.zeros((1, blk), F32)
            else:
                b_n = st["bias_t"][n:n + 1, :]
            s_ref[slot, rows, :] = sc
            m_n = jnp.max(sc, axis=0, keepdims=True) + b_n
            st["m"] = m_n if st["m"] is None else jnp.maximum(st["m"], m_n)
            st["biases"].append(b_n)

        return [prologue] + [functools.partial(block, n) for n in range(qi + 1)]

    def prob_steps(u):
        qi, _ = units[u]
        slot = u % n_slots
        st = state[u]

        def block(n):
            rows = slice(n * blk, (n + 1) * blk)
            p_ref[slot, rows, :] = jnp.exp2(s_ref[slot, rows, :] - (st["m"] - st["biases"][n])).astype(BF16)

        return [functools.partial(block, n) for n in range(qi + 1)]

    def value_steps(u):
        qi, h = units[u]
        slot = u % n_slots
        st = state[u]
        st["acc"] = None

        def block(n):
            rows = slice(n * blk, (n + 1) * blk)
            part = _dot(vt_ref[h, :, rows], p_ref[slot, rows, :])
            st["acc"] = part if st["acc"] is None else st["acc"] + part

        def epilogue():
            acc = st["acc"]
            st["out"] = acc[0:hd] * (1.0 / acc[hd:hd + 1])
            if h == 1:
                pair = jnp.concatenate([state[u - 1]["out"], st["out"]], axis=0)
                o_ref[qi * blk:(qi + 1) * blk, :] = pair.T.astype(BF16)
                del state[u - 1]

        return [functools.partial(block, n) for n in range(qi + 1)] + [epilogue]

    width = MOBA_SLOTS // 3

    def stage(make_steps, first):
        return [make_steps(u) for u in range(first, first + width) if 0 <= u < len(units)]

    for t in range(0, len(units) + 2 * width, width):
        streams = (stage(score_steps, t) + stage(prob_steps, t - width) +
                   stage(value_steps, t - 2 * width))
        for i in range(max(len(stream) for stream in streams)):
            for stream in streams:
                if i < len(stream):
                    stream[i]()


def _moba(q, k, v):
    b, s, _ = q.shape
    spec = pl.BlockSpec((None, s, LANES), lambda bi, hi: (bi, 0, hi))
    return pl.pallas_call(
        _moba_kernel,
        grid=(b, ATT_N_HEADS // 2),
        in_specs=[spec, spec, pl.BlockSpec((None, LANES, s), lambda bi, hi: (bi, hi, 0))],
        out_specs=spec,
        out_shape=jax.ShapeDtypeStruct((b, s, ATT_WIDTH), BF16),
        scratch_shapes=[pltpu.VMEM((2, VT_ROWS, s), BF16), pltpu.VMEM((MOBA_SLOTS, s, MOBA_BLOCK), F32),
                        pltpu.VMEM((MOBA_SLOTS, s, MOBA_BLOCK), BF16)],
        compiler_params=pltpu.CompilerParams(
            dimension_semantics=("arbitrary", "arbitrary"), vmem_limit_bytes=VMEM_LIMIT_SMALL),
        name="moba_attention",
    )(q, k, v)


def _mixer_kernel(x_ref, ys_ref, ya_ref, g_ref, bg_ref, wsp_ref, wap_ref, wo_ref, n2_ref, wup_ref, fcw_ref,
                  fcb_ref, wdn_ref, nf_ref, o_ref, up_ref):
    tm = x_ref.shape[0]
    halo = SUBLANES

    @pl.when(pl.program_id(1) == 0)
    def _():
        up_ref[:, 0:halo, :] = jnp.zeros((up_ref.shape[0], halo, LANES), F32)

    g_ssd = _sigmoid(g_ref[:, :D_MODEL].astype(F32) + bg_ref[:, :D_MODEL])
    g_att = _sigmoid(g_ref[:, D_MODEL:].astype(F32) + bg_ref[:, D_MODEL:])
    merged = g_ssd * _dot(ys_ref[...], wsp_ref[...]) + g_att * _dot(ya_ref[...], wap_ref[...])
    x1 = x_ref[...] + _dot(merged.astype(BF16), wo_ref[...])

    var = jnp.mean(x1 * x1, axis=-1, keepdims=True)
    h2 = (x1 * lax.rsqrt(var + RMS_EPS) * n2_ref[...]).astype(BF16)

    up = _dot(h2, wup_ref[:, 0:FFN_HIDDEN])
    convs = []
    for tile in range(FFN_HIDDEN // LANES):
        lanes = slice(tile * LANES, (tile + 1) * LANES)
        up_ref[tile, halo:halo + tm, :] = up[:, lanes]
        conv = fcb_ref[:, lanes]
        for k in range(FFN_CONV):
            off = halo - (FFN_CONV - 1) + k
            conv = conv + fcw_ref[k:k + 1, lanes] * up_ref[tile, off:off + tm, :]
        up_ref[tile, 0:halo, :] = up_ref[tile, tm:tm + halo, :]
        convs.append(conv)
    conv = jnp.concatenate(convs, axis=1)
    gate = _dot(h2, wup_ref[:, FFN_HIDDEN:2 * FFN_HIDDEN])
    act = (conv * _sigmoid(conv) * gate).astype(BF16)
    y = x1 + _dot(act, wdn_ref[...])

    var = jnp.mean(y * y, axis=-1, keepdims=True)
    o_ref[...] = y * lax.rsqrt(var + RMS_EPS) * nf_ref[...]


def _mixer(x, y_ssd, y_att, gates, b_gate, w_ssd_proj, w_att_proj, w_out, norm2_w, w_ffn_up, ffn_conv_w, ffn_conv_b,
           w_ffn_down, norm_f_w):
    b, s, d = x.shape
    tm = ROW_TILE
    row = lambda bi, ti: (bi, ti, 0)
    wsp, wap, wo, wup, wdn = (w.astype(BF16) for w in (w_ssd_proj, w_att_proj, w_out, w_ffn_up, w_ffn_down))
    return pl.pallas_call(
        _mixer_kernel,
        grid=(b, s // tm),
        in_specs=[
            pl.BlockSpec((None, tm, d), row),
            pl.BlockSpec((None, tm, SSD_D_INNER), row),
            pl.BlockSpec((None, tm, ATT_WIDTH), row),
            pl.BlockSpec((None, tm, 2 * D_MODEL), row),
            _resident((1, 2 * D_MODEL)),
            _resident(wsp.shape), _resident(wap.shape), _resident(wo.shape), _resident((1, d)),
            _resident(wup.shape), _resident((FFN_CONV, FFN_HIDDEN)), _resident((1, FFN_HIDDEN)),
            _resident(wdn.shape), _resident((1, d)),
        ],
        out_specs=pl.BlockSpec((None, tm, d), row),
        out_shape=jax.ShapeDtypeStruct((b, s, d), F32),
        scratch_shapes=[pltpu.VMEM((FFN_HIDDEN // LANES, tm + SUBLANES, LANES), F32)],
        compiler_params=pltpu.CompilerParams(
            dimension_semantics=("arbitrary", "arbitrary"), vmem_limit_bytes=VMEM_LIMIT_BIG),
        name="merge_convffn",
    )(x, y_ssd, y_att, gates, b_gate.reshape(1, -1), wsp, wap, wo, norm2_w.reshape(1, d), wup, ffn_conv_w,
      ffn_conv_b.reshape(1, -1), wdn, norm_f_w.reshape(1, d))


def kernel(x, norm1_w, w_in, b_gate, ssd_conv_w, ssd_conv_b, ssd_dt_bias, ssd_a_log, ssd_d, ssd_norm_w,
           w_ssd_proj, w_att_proj, w_out, norm2_w, w_ffn_up, ffn_conv_w, ffn_conv_b, w_ffn_down, norm_f_w):
    assert norm1_w.shape[0] == 1, "single-layer block"
    sz, xbc, dt, dtT, q, k, v, gates = _in_projection(
        x, norm1_w[0], w_in[0], ssd_conv_w[0], ssd_conv_b[0], ssd_dt_bias[0])
    y_ssd = _ssd(xbc, sz, dt, dtT, ssd_a_log[0], ssd_d[0], ssd_norm_w[0])
    y_att = _moba(q, k, v)
    return _mixer(x, y_ssd, y_att, gates, b_gate[0], w_ssd_proj[0], w_att_proj[0], w_out[0], norm2_w[0], w_ffn_up[0],
                  ffn_conv_w[0], ffn_conv_b[0], w_ffn_down[0], norm_f_w)
```

```python
import functools

import jax
import jax.numpy as jnp
import numpy as np
from jax import lax
from jax.experimental import pallas as pl
from jax.experimental.pallas import tpu as pltpu

D_MODEL = 1024
RMS_EPS = 1e-6
NEG_INF = -1e30

SSD_D_INNER = 2048
SSD_HEAD_DIM = 64
SSD_N_HEADS = 32
SSD_N_GROUPS = 8
SSD_HEADS_PER_GROUP = SSD_N_HEADS // SSD_N_GROUPS
SSD_GROUP_WIDTH = SSD_D_INNER // SSD_N_GROUPS
SSD_D_STATE = 128
SSD_CONV = 4
SSD_CHUNK = 128
SSD_GN = SSD_N_GROUPS * SSD_D_STATE
SSD_CONV_DIM = SSD_D_INNER + 2 * SSD_GN

ATT_HEAD_DIM = 64
ATT_N_HEADS = 16
ATT_WIDTH = ATT_N_HEADS * ATT_HEAD_DIM
MOBA_BLOCK = 256
MOBA_TOPK = 3
ROPE_THETA = 500000.0
ROT_DIM = ATT_HEAD_DIM // 4

FFN_HIDDEN = 2816
FFN_CONV = 3

IN_SIZES = (SSD_D_INNER, SSD_CONV_DIM, SSD_N_HEADS, ATT_WIDTH, ATT_WIDTH, ATT_WIDTH, 2 * D_MODEL)
IN_OFFSETS = tuple(int(v) for v in np.cumsum(IN_SIZES)[:-1])

LANES = 128
SUBLANES = 8
VMEM_LIMIT_BIG = 56 * 1024 * 1024
VMEM_LIMIT_SMALL = 40 * 1024 * 1024

ROW_TILE = 256
COL_CHUNK = 512
CONV_STAGE_SLOTS = 2 * COL_CHUNK // LANES

BF16 = jnp.bfloat16
F32 = jnp.float32


def _dot(a, b):
    return jnp.dot(a, b, preferred_element_type=F32)


def _dot_nt(a, b):
    return lax.dot_general(a, b, (((1,), (1,)), ((), ())), preferred_element_type=F32)


def _dot_tn(a, b):
    return lax.dot_general(a, b, (((0,), (0,)), ((), ())), preferred_element_type=F32)


def _sigmoid(x):
    return 1.0 / (1.0 + jnp.exp2(x * float(-np.log2(np.e))))


def _softplus(x):
    return jnp.maximum(x, 0.0) + jnp.log1p(jnp.exp(-jnp.abs(x)))


def _split3(x):
    hi = x.astype(BF16)
    r = x - hi.astype(F32)
    mid = r.astype(BF16)
    lo = (r - mid.astype(F32)).astype(BF16)
    return hi, mid, lo


def _resident(shape):
    nd = len(shape)
    return pl.BlockSpec(shape, lambda *_: (0,) * nd, pipeline_mode=pl.Buffered(1))


def _inproj_kernel(x_ref, n1_ref, wz_ref, wxbc_ref, wdt_ref, wdtT_ref, wq_ref, wk_ref, wv_ref, wg_ref,
                   cw_ref, cb_ref, dtb_ref, dtbT_ref, cos_ref, sa_ref, sb_ref,
                   sz_ref, xbc_ref, dt_ref, dtT_ref, q_ref, k_ref, v_ref, g_ref, raw_ref, wcopy_ref, stage_ref):
    tm = x_ref.shape[0]
    halo = SUBLANES

    @pl.when(pl.program_id(1) == 0)
    def _():
        raw_ref[...] = jnp.zeros(raw_ref.shape, F32)

    @pl.when((pl.program_id(0) == 0) & (pl.program_id(1) == 0))
    def _():
        wcopy_ref[...] = wxbc_ref[...]

    xf = x_ref[...]
    var = jnp.mean(xf * xf, axis=-1, keepdims=True)
    hn = (xf * lax.rsqrt(var + RMS_EPS) * n1_ref[...]).astype(BF16)

    for c in range(0, SSD_D_INNER, COL_CHUNK):
        acc = _dot(hn, wz_ref[:, c:c + COL_CHUNK])
        sz_ref[:, c:c + COL_CHUNK] = (acc * _sigmoid(acc)).astype(BF16)

    for c in range(0, SSD_CONV_DIM, COL_CHUNK):
        cs = slice(c, c + COL_CHUNK)
        acc = _dot(hn, wcopy_ref[:, cs])
        for j in range(COL_CHUNK // LANES):
            lanes = slice(c + j * LANES, c + (j + 1) * LANES)
            t = acc[:, j * LANES:(j + 1) * LANES]
            col = (c // LANES + j) % stage_ref.shape[0]
            stage_ref[col, 0:halo, :] = raw_ref[:, lanes]
            stage_ref[col, halo:halo + tm, :] = t
            conv = cb_ref[:, lanes] + cw_ref[SSD_CONV - 1:SSD_CONV, lanes] * t
            for k in range(1, SSD_CONV):
                conv = conv + (cw_ref[SSD_CONV - 1 - k:SSD_CONV - k, lanes] *
                               stage_ref[col, halo - k:halo - k + tm, :])
            xbc_ref[:, lanes] = (conv * _sigmoid(conv)).astype(BF16)
            raw_ref[:, lanes] = t[tm - halo:tm, :]

    dt_ref[...] = _softplus(_dot(hn, wdt_ref[...]) + dtb_ref[...])
    dtT_ref[...] = _softplus(_dot_nt(wdtT_ref[...], hn) + dtbT_ref[...])

    cos, sa, sb = cos_ref[...], sa_ref[...], sb_ref[...]
    half = ROT_DIM // 2
    for w_ref, o_ref in ((wq_ref, q_ref), (wk_ref, k_ref)):
        for c in range(0, ATT_WIDTH, COL_CHUNK):
            acc = _dot(hn, w_ref[:, c:c + COL_CHUNK])
            for j in range(0, COL_CHUNK, LANES):
                t = acc[:, j:j + LANES]
                nxt = pltpu.roll(t, LANES - half, axis=1)
                prv = pltpu.roll(t, half, axis=1)
                rot = t * cos + nxt * sa + prv * sb
                if o_ref is q_ref:
                    o_ref[c + j:c + j + LANES, :] = (rot * MOBA_Q_SCALE).T.astype(BF16)
                else:
                    o_ref[:, c + j:c + j + LANES] = rot.astype(BF16)
    for c in range(0, ATT_WIDTH, COL_CHUNK):
        acc = _dot(hn, wv_ref[:, c:c + COL_CHUNK])
        for j in range(0, COL_CHUNK, LANES):
            v_ref[c + j:c + j + LANES, :] = acc[:, j:j + LANES].T.astype(BF16)

    for c in range(0, 2 * D_MODEL, COL_CHUNK):
        g_ref[:, c:c + COL_CHUNK] = _dot(hn, wg_ref[:, c:c + COL_CHUNK]).astype(BF16)


def _rope_tables(s):
    half = ROT_DIM // 2
    pos = jnp.arange(s, dtype=F32)
    inv_freq = jnp.float32(ROPE_THETA) ** (-jnp.arange(0, ROT_DIM, 2, dtype=F32) / ROT_DIM)
    ang = pos[:, None] * inv_freq[None, :]
    ang = jnp.concatenate([ang, ang], axis=-1)
    cos, sin = jnp.cos(ang), jnp.sin(ang)
    pad = ATT_HEAD_DIM - ROT_DIM
    cos_t = jnp.concatenate([cos, jnp.ones((s, pad), F32)], axis=-1)
    sa_t = jnp.concatenate([-sin[:, :half], jnp.zeros((s, ATT_HEAD_DIM - half), F32)], axis=-1)
    sb_t = jnp.concatenate([jnp.zeros((s, half), F32), sin[:, half:], jnp.zeros((s, pad), F32)], axis=-1)
    reps = LANES // ATT_HEAD_DIM
    return tuple(jnp.tile(t, (1, reps)) for t in (cos_t, sa_t, sb_t))


def _in_projection(x, norm1_w, w_in, conv_w, conv_b, dt_bias):
    b, s, d = x.shape
    tm = ROW_TILE
    wz, wxbc, wdt, wq, wk, wv, wg = jnp.split(w_in.astype(BF16), IN_OFFSETS, axis=-1)
    cos_t, sa_t, sb_t = _rope_tables(s)
    row = lambda bi, ti: (bi, ti, 0)
    tab = lambda bi, ti: (ti, 0)

    def out(width, dtype=BF16):
        return jax.ShapeDtypeStruct((b, s, width), dtype)

    in_specs = [
        pl.BlockSpec((None, tm, d), row),
        _resident((1, d)),
        _resident(wz.shape), _resident(wxbc.shape), _resident(wdt.shape), _resident((SSD_N_HEADS, d)),
        _resident(wq.shape), _resident(wk.shape), _resident(wv.shape), _resident(wg.shape),
        _resident((SSD_CONV, SSD_CONV_DIM)), _resident((1, SSD_CONV_DIM)),
        _resident((1, SSD_N_HEADS)), _resident((SSD_N_HEADS, 1)),
        pl.BlockSpec((tm, LANES), tab), pl.BlockSpec((tm, LANES), tab), pl.BlockSpec((tm, LANES), tab),
    ]
    out_specs = [
        pl.BlockSpec((None, tm, SSD_D_INNER), row),
        pl.BlockSpec((None, tm, SSD_CONV_DIM), row),
        pl.BlockSpec((None, tm, SSD_N_HEADS), row),
        pl.BlockSpec((None, SSD_N_HEADS, tm), lambda bi, ti: (bi, 0, ti)),
        pl.BlockSpec((None, ATT_WIDTH, tm), lambda bi, ti: (bi, 0, ti)),
        pl.BlockSpec((None, tm, ATT_WIDTH), row),
        pl.BlockSpec((None, ATT_WIDTH, tm), lambda bi, ti: (bi, 0, ti)),
        pl.BlockSpec((None, tm, 2 * D_MODEL), row),
    ]
    out_shape = [out(SSD_D_INNER), out(SSD_CONV_DIM), out(SSD_N_HEADS, F32),
                 jax.ShapeDtypeStruct((b, SSD_N_HEADS, s), F32),
                 jax.ShapeDtypeStruct((b, ATT_WIDTH, s), BF16), out(ATT_WIDTH),
                 jax.ShapeDtypeStruct((b, ATT_WIDTH, s), BF16), out(2 * D_MODEL)]
    return pl.pallas_call(
        _inproj_kernel,
        grid=(b, s // tm),
        in_specs=in_specs,
        out_specs=out_specs,
        out_shape=out_shape,
        scratch_shapes=[pltpu.VMEM((SUBLANES, SSD_CONV_DIM), F32), pltpu.VMEM(wxbc.shape, BF16),
                        pltpu.VMEM((CONV_STAGE_SLOTS, tm + SUBLANES, LANES), F32)],
        compiler_params=pltpu.CompilerParams(
            dimension_semantics=("arbitrary", "arbitrary"), vmem_limit_bytes=VMEM_LIMIT_BIG),
        name="in_projection",
    )(x, norm1_w.reshape(1, d), wz, wxbc, wdt, wdt.T, wq, wk, wv, wg,
      conv_w, conv_b.reshape(1, -1),
      dt_bias.reshape(1, -1), dt_bias.reshape(-1, 1), cos_t, sa_t, sb_t)


def _ssd_kernel(xbc_ref, sz_ref, dt_ref, dtT_ref, alog_ref, alogT_ref, dsk_ref, nw_ref, e_ch_ref,
                y_ref, state_ref):
    l = SSD_CHUNK
    gw = SSD_GROUP_WIDTH
    hpg = SSD_HEADS_PER_GROUP
    p = SSD_HEAD_DIM

    @pl.when(pl.program_id(1) == 0)
    def _():
        state_ref[...] = jnp.zeros(state_ref.shape, F32)

    log2e = float(np.log2(np.e))
    dt = dt_ref[...]
    a = dt * (-log2e * jnp.exp(alog_ref[...]))
    aT = dtT_ref[...] * (-log2e * jnp.exp(alogT_ref[...]))

    ri = lax.broadcasted_iota(jnp.int32, (l, l), 0)
    ci = lax.broadcasted_iota(jnp.int32, (l, l), 1)
    lower = ri >= ci
    tri = jnp.where(lower, 1.0, 0.0).astype(BF16)
    triT = jnp.where(ri <= ci, 1.0, 0.0).astype(BF16)
    cs = sum(_dot(tri, piece) for piece in _split3(a))
    csT = sum(_dot(piece, triT) for piece in _split3(aT))
    dt_pieces = jnp.concatenate(_split3(dt), axis=1)
    cs_pieces = jnp.concatenate(_split3(cs), axis=1)

    lane_head = lax.broadcasted_iota(jnp.int32, (l, gw), 1) // p

    def group_steps(g):
        cols = slice(g * gw, (g + 1) * gw)
        v = {}

        def load_and_project():
            v["xg"] = xbc_ref[:, cols].astype(F32)
            v["bg"] = xbc_ref[:, SSD_D_INNER + g * SSD_D_STATE:SSD_D_INNER + (g + 1) * SSD_D_STATE]
            v["cg"] = xbc_ref[:, SSD_D_INNER + SSD_GN + g * SSD_D_STATE:
                              SSD_D_INNER + SSD_GN + (g + 1) * SSD_D_STATE]
            v["cb"] = _dot_nt(v["cg"], v["bg"])
            v["dt_g"] = _dot(dt_pieces, e_ch_ref[:, cols])
            v["cs_g"] = _dot(cs_pieces, e_ch_ref[:, cols])
            v["st"] = state_ref[g]
            v["y_off"] = _dot(v["cg"], v["st"].astype(BF16))

        def scale_inputs():
            tot_g = v["tot_g"] = v["cs_g"][l - 1:l, :]
            v["xdt"] = v["xg"] * v["dt_g"]
            v["xdec"] = (v["xdt"] * jnp.exp2(tot_g - v["cs_g"])).astype(BF16)
            v["y"] = v["xg"] * dsk_ref[:, cols] + v["y_off"] * jnp.exp2(v["cs_g"])

        def head_pair(r0):
            mix, rhs = [], []
            for r in (r0, r0 + 1):
                h = g * hpg + r
                decay = jnp.exp2(cs[:, h:h + 1] - csT[h:h + 1, :])
                mix.append(jnp.where(lower, v["cb"] * decay, 0.0).astype(BF16))
                rhs.append(jnp.where(lane_head == r, v["xdt"], 0.0).astype(BF16))
            v["y"] = v["y"] + _dot(jnp.concatenate(mix, axis=1), jnp.concatenate(rhs, axis=0))

        def update_state():
            state_ref[g] = v["st"] * jnp.exp2(v["tot_g"]) + _dot_tn(v["bg"], v["xdec"])

        def gate_and_norm():
            y = v["y"] * sz_ref[:, cols].astype(F32)
            ms = jnp.mean(y * y, axis=-1, keepdims=True)
            y_ref[:, cols] = (y * lax.rsqrt(ms + RMS_EPS) * nw_ref[:, cols]).astype(BF16)

        return ([load_and_project, scale_inputs] +
                [functools.partial(head_pair, r0) for r0 in range(0, hpg, 2)] +
                [update_state, gate_and_norm])

    pending = [group_steps(g) for g in range(SSD_N_GROUPS)]
    n_steps = len(pending[0])
    for t in range(SSD_N_GROUPS + n_steps - 1):
        for g in range(SSD_N_GROUPS):
            if 0 <= t - g < n_steps:
                pending[g][t - g]()


def _ssd(xbc, sz, dt, dtT, a_log, d_skip, norm_w):
    b, s, _ = xbc.shape
    l = SSD_CHUNK
    row = lambda bi, ci: (bi, ci, 0)
    piece_head = np.arange(3 * SSD_N_HEADS)[:, None] % SSD_N_HEADS
    e_ch = jnp.asarray(piece_head == np.arange(SSD_D_INNER)[None, :] // SSD_HEAD_DIM, BF16)
    return pl.pallas_call(
        _ssd_kernel,
        grid=(b, s // l),
        in_specs=[
            pl.BlockSpec((None, l, SSD_CONV_DIM), row),
            pl.BlockSpec((None, l, SSD_D_INNER), row),
            pl.BlockSpec((None, l, SSD_N_HEADS), row),
            pl.BlockSpec((None, SSD_N_HEADS, l), lambda bi, ci: (bi, 0, ci)),
            _resident((1, SSD_N_HEADS)), _resident((SSD_N_HEADS, 1)),
            _resident((1, SSD_D_INNER)), _resident((1, SSD_D_INNER)),
            _resident(e_ch.shape),
        ],
        out_specs=pl.BlockSpec((None, l, SSD_D_INNER), row),
        out_shape=jax.ShapeDtypeStruct((b, s, SSD_D_INNER), BF16),
        scratch_shapes=[pltpu.VMEM((SSD_N_GROUPS, SSD_D_STATE, SSD_GROUP_WIDTH), F32)],
        compiler_params=pltpu.CompilerParams(
            dimension_semantics=("arbitrary", "arbitrary"), vmem_limit_bytes=VMEM_LIMIT_SMALL),
        name="ssd_scan",
    )(xbc, sz, dt, dtT, a_log.reshape(1, -1), a_log.reshape(-1, 1),
      jnp.repeat(d_skip, SSD_HEAD_DIM).reshape(1, -1), norm_w.reshape(1, -1), e_ch)


MOBA_Q_SCALE = ATT_HEAD_DIM ** -0.5 * float(np.log2(np.e))
VT_ROWS = ATT_HEAD_DIM + 16
MOBA_SLOTS = 6


def _moba_kernel(q_ref, k_ref, v_ref, o_ref, vt_ref, s_ref, p_ref):
    s = k_ref.shape[0]
    blk = MOBA_BLOCK
    nb = s // blk
    nbp = -(-nb // SUBLANES) * SUBLANES
    hd = ATT_HEAD_DIM

    for h in range(2):
        vt_ref[h, 0:hd, :] = v_ref[h * hd:(h + 1) * hd, :]
    vt_ref[:, hd:VT_ROWS, :] = jnp.ones((2, VT_ROWS - hd, s), BF16)

    arow = lax.broadcasted_iota(jnp.int32, (nbp, s), 0)
    acol = lax.broadcasted_iota(jnp.int32, (nbp, s), 1) // blk
    avg = jnp.where(arow == acol, 1.0 / blk, 0.0).astype(BF16)
    km = _dot(avg, k_ref[...])
    km_hi = km.astype(BF16)
    km_lo = (km - km_hi.astype(F32)).astype(BF16)

    head_row = lax.broadcasted_iota(jnp.int32, (LANES, blk), 0) // hd
    n_idx = lax.broadcasted_iota(jnp.int32, (nbp, blk), 0)
    causal_t = (lax.broadcasted_iota(jnp.int32, (blk, blk), 0) <=
                lax.broadcasted_iota(jnp.int32, (blk, blk), 1))

    n_slots = s_ref.shape[0]
    units = [(qi, h) for qi in range(nb) for h in range(2)]
    state = {}

    def score_steps(u):
        qi, h = units[u]
        slot = u % n_slots
        st = state[u] = {"biases": [], "m": None}

        def prologue():
            if h == 0:
                st["qt"] = q_ref[:, qi * blk:(qi + 1) * blk].astype(F32)
            else:
                st["qt"] = state[u - 1]["qt"]
            qt_h = st["qt_h"] = jnp.where(head_row == h, st["qt"], 0.0).astype(BF16)
            if qi > 0:
                gt = _dot(km_hi, qt_h) + _dot(km_lo, qt_h)
                rank = jnp.zeros((nbp, blk), jnp.int32)
                for m in range(qi):
                    gm = gt[m:m + 1, :]
                    beats = (gm > gt) | ((gm == gt) & (n_idx > m))
                    rank = rank + jnp.where(beats, 1, 0)
                st["bias_t"] = jnp.where(rank >= MOBA_TOPK, NEG_INF, 0.0)

        def block(n):
            rows = slice(n * blk, (n + 1) * blk)
            sc = _dot(k_ref[rows, :], st["qt_h"])
            if n == qi:
                sc = jnp.where(causal_t, sc, NEG_INF)
                b_n = jnp.zeros((1, blk), F32)
            else:
                b_n = st["bias_t"][n:n + 1, :]
            s_ref[slot, rows, :] = sc
            m_n = jnp.max(sc, axis=0, keepdims=True) + b_n
            st["m"] = m_n if st["m"] is None else jnp.maximum(st["m"], m_n)
            st["biases"].append(b_n)

        return [prologue] + [functools.partial(block, n) for n in range(qi + 1)]

    def prob_steps(u):
        qi, _ = units[u]
        slot = u % n_slots
        st = state[u]

        def block(n):
            rows = slice(n * blk, (n + 1) * blk)
            p_ref[slot, rows, :] = jnp.exp2(s_ref[slot, rows, :] - (st["m"] - st["biases"][n])).astype(BF16)

        return [functools.partial(block, n) for n in range(qi + 1)]

    def value_steps(u):
        qi, h = units[u]
        slot = u % n_slots
        st = state[u]
        st["acc"] = None

        def block(n):
            rows = slice(n * blk, (n + 1) * blk)
            part = _dot(vt_ref[h, :, rows], p_ref[slot, rows, :])
            st["acc"] = part if st["acc"] is None else st["acc"] + part

        def epilogue():
            acc = st["acc"]
            st["out"] = acc[0:hd] * (1.0 / acc[hd:hd + 1])
            if h == 1:
                pair = jnp.concatenate([state[u - 1]["out"], st["out"]], axis=0)
                o_ref[qi * blk:(qi + 1) * blk, :] = pair.T.astype(BF16)
                del state[u - 1]

        return [functools.partial(block, n) for n in range(qi + 1)] + [epilogue]

    width = MOBA_SLOTS // 3

    def stage(make_steps, first):
        return [make_steps(u) for u in range(first, first + width) if 0 <= u < len(units)]

    for t in range(0, len(units) + 2 * width, width):
        streams = (stage(score_steps, t) + stage(prob_steps, t - width) +
                   stage(value_steps, t - 2 * width))
        for i in range(max(len(stream) for stream in streams)):
            for stream in streams:
                if i < len(stream):
                    stream[i]()


def _moba(q, k, v):
    b, s, _ = k.shape
    spec = pl.BlockSpec((None, s, LANES), lambda bi, hi: (bi, 0, hi))
    spec_t = pl.BlockSpec((None, LANES, s), lambda bi, hi: (bi, hi, 0))
    return pl.pallas_call(
        _moba_kernel,
        grid=(b, ATT_N_HEADS // 2),
        in_specs=[spec_t, spec, spec_t],
        out_specs=spec,
        out_shape=jax.ShapeDtypeStruct((b, s, ATT_WIDTH), BF16),
        scratch_shapes=[pltpu.VMEM((2, VT_ROWS, s), BF16), pltpu.VMEM((MOBA_SLOTS, s, MOBA_BLOCK), F32),
                        pltpu.VMEM((MOBA_SLOTS, s, MOBA_BLOCK), BF16)],
        compiler_params=pltpu.CompilerParams(
            dimension_semantics=("arbitrary", "arbitrary"), vmem_limit_bytes=VMEM_LIMIT_SMALL),
        name="moba_attention",
    )(q, k, v)


def _mixer_kernel(x_ref, ys_ref, ya_ref, g_ref, bg_ref, wsp_ref, wap_ref, wo_ref, n2_ref, wup_ref, fcw_ref,
                  fcb_ref, wdn_ref, nf_ref, o_ref, up_ref):
    tm = x_ref.shape[0]
    halo = SUBLANES

    @pl.when(pl.program_id(1) == 0)
    def _():
        up_ref[:, 0:halo, :] = jnp.zeros((up_ref.shape[0], halo, LANES), F32)

    g_ssd = _sigmoid(g_ref[:, :D_MODEL].astype(F32) + bg_ref[:, :D_MODEL])
    g_att = _sigmoid(g_ref[:, D_MODEL:].astype(F32) + bg_ref[:, D_MODEL:])
    merged = g_ssd * _dot(ys_ref[...], wsp_ref[...]) + g_att * _dot(ya_ref[...], wap_ref[...])
    x1 = x_ref[...] + _dot(merged.astype(BF16), wo_ref[...])

    var = jnp.mean(x1 * x1, axis=-1, keepdims=True)
    h2 = (x1 * lax.rsqrt(var + RMS_EPS) * n2_ref[...]).astype(BF16)

    up = _dot(h2, wup_ref[:, 0:FFN_HIDDEN])
    convs = []
    for tile in range(FFN_HIDDEN // LANES):
        lanes = slice(tile * LANES, (tile + 1) * LANES)
        up_ref[tile, halo:halo + tm, :] = up[:, lanes]
        conv = fcb_ref[:, lanes]
        for k in range(FFN_CONV):
            off = halo - (FFN_CONV - 1) + k
            conv = conv + fcw_ref[k:k + 1, lanes] * up_ref[tile, off:off + tm, :]
        up_ref[tile, 0:halo, :] = up_ref[tile, tm:tm + halo, :]
        convs.append(conv)
    conv = jnp.concatenate(convs, axis=1)
    gate = _dot(h2, wup_ref[:, FFN_HIDDEN:2 * FFN_HIDDEN])
    act = (conv * _sigmoid(conv) * gate).astype(BF16)
    y = x1 + _dot(act, wdn_ref[...])

    var = jnp.mean(y * y, axis=-1, keepdims=True)
    o_ref[...] = y * lax.rsqrt(var + RMS_EPS) * nf_ref[...]


def _mixer(x, y_ssd, y_att, gates, b_gate, w_ssd_proj, w_att_proj, w_out, norm2_w, w_ffn_up, ffn_conv_w, ffn_conv_b,
           w_ffn_down, norm_f_w):
    b, s, d = x.shape
    tm = ROW_TILE
    row = lambda bi, ti: (bi, ti, 0)
    wsp, wap, wo, wup, wdn = (w.astype(BF16) for w in (w_ssd_proj, w_att_proj, w_out, w_ffn_up, w_ffn_down))
    return pl.pallas_call(
        _mixer_kernel,
        grid=(b, s // tm),
        in_specs=[
            pl.BlockSpec((None, tm, d), row),
            pl.BlockSpec((None, tm, SSD_D_INNER), row),
            pl.BlockSpec((None, tm, ATT_WIDTH), row),
            pl.BlockSpec((None, tm, 2 * D_MODEL), row),
            _resident((1, 2 * D_MODEL)),
            _resident(wsp.shape), _resident(wap.shape), _resident(wo.shape), _resident((1, d)),
            _resident(wup.shape), _resident((FFN_CONV, FFN_HIDDEN)), _resident((1, FFN_HIDDEN)),
            _resident(wdn.shape), _resident((1, d)),
        ],
        out_specs=pl.BlockSpec((None, tm, d), row),
        out_shape=jax.ShapeDtypeStruct((b, s, d), F32),
        scratch_shapes=[pltpu.VMEM((FFN_HIDDEN // LANES, tm + SUBLANES, LANES), F32)],
        compiler_params=pltpu.CompilerParams(
            dimension_semantics=("arbitrary", "arbitrary"), vmem_limit_bytes=VMEM_LIMIT_BIG),
        name="merge_convffn",
    )(x, y_ssd, y_att, gates, b_gate.reshape(1, -1), wsp, wap, wo, norm2_w.reshape(1, d), wup, ffn_conv_w,
      ffn_conv_b.reshape(1, -1), wdn, norm_f_w.reshape(1, d))


def kernel(x, norm1_w, w_in, b_gate, ssd_conv_w, ssd_conv_b, ssd_dt_bias, ssd_a_log, ssd_d, ssd_norm_w,
           w_ssd_proj, w_att_proj, w_out, norm2_w, w_ffn_up, ffn_conv_w, ffn_conv_b, w_ffn_down, norm_f_w):
    assert norm1_w.shape[0] == 1, "single-layer block"
    sz, xbc, dt, dtT, q, k, v, gates = _in_projection(
        x, norm1_w[0], w_in[0], ssd_conv_w[0], ssd_conv_b[0], ssd_dt_bias[0])
    y_ssd = _ssd(xbc, sz, dt, dtT, ssd_a_log[0], ssd_d[0], ssd_norm_w[0])
    y_att = _moba(q, k, v)
    return _mixer(x, y_ssd, y_att, gates, b_gate[0], w_ssd_proj[0], w_att_proj[0], w_out[0], norm2_w[0], w_ffn_up[0],
                  ffn_conv_w[0], ffn_conv_b[0], w_ffn_down[0], norm_f_w)
```
